```python
import jax, jax.numpy as jnp
from jax import lax
import numpy as np

D_MODEL = 1024
BATCH = 2
SEQ = 8192
DEPTH = 4

HEAD_DIM = 64
D_MIX = D_MODEL
A_WIDTH = D_MIX // 4
A_HEADS = A_WIDTH // HEAD_DIM
B_WIDTH = D_MIX // 2
B_HEADS = B_WIDTH // HEAD_DIM
B_KV_HEADS = B_HEADS // 4
B_GROUP = B_HEADS // B_KV_HEADS
WINDOW = 128
C_WIDTH = D_MIX // 4
C_HEADS = C_WIDTH // HEAD_DIM
CHUNK = 64
D_FF = (11 * D_MODEL) // 4
CONV_WIDTH = 3
PLE_DIM = 256
ROPE_THETA = 10000.0
EPS = 1e-6

PART_SIZES = [A_WIDTH, A_WIDTH, A_WIDTH, A_WIDTH,
              B_WIDTH, B_KV_HEADS * HEAD_DIM, B_KV_HEADS * HEAD_DIM,
              C_WIDTH, C_WIDTH, C_WIDTH, C_WIDTH]
D_IN = int(sum(PART_SIZES))
SPLITS = [int(v) for v in np.cumsum(PART_SIZES)[:-1]]

kernel_name = "hybrid_hgrn2_swa_retention_trunk"


def _rmsnorm(x, gain=None):
    x32 = x.astype(jnp.float32)
    y = x32 * lax.rsqrt(jnp.mean(x32 * x32, axis=-1, keepdims=True) + EPS)
    if gain is not None:
        y = y * gain.astype(jnp.float32)
    return y


def _rope_tables(positions):
    inv = 1.0 / (ROPE_THETA ** (jnp.arange(0, HEAD_DIM, 2, dtype=jnp.float32) / HEAD_DIM))
    ang = positions.astype(jnp.float32)[..., None] * inv
    ang = jnp.concatenate([ang, ang], axis=-1)
    return jnp.cos(ang)[:, :, None, :], jnp.sin(ang)[:, :, None, :]


def _apply_rope(x, cos, sin):
    x = x.astype(jnp.float32)
    x1, x2 = jnp.split(x, 2, axis=-1)
    return x * cos + jnp.concatenate([-x2, x1], axis=-1) * sin


def _chunk(t):
    b, s, h, d = t.shape
    return t.reshape(b, s // CHUNK, CHUNK, h, d).transpose(0, 3, 1, 2, 4)


def _unchunk(t):
    b, h, n, c, d = t.shape
    return t.transpose(0, 2, 3, 1, 4).reshape(b, n * c, h, d)


def _hgrn2(q, f_logit, inp, g, lb, gnorm_gain):
    b, s, h, d = q.shape
    lb = lb.astype(jnp.float32).reshape(h, d)
    fl = f_logit.astype(jnp.float32)
    log_f = jnp.log(lb + (1.0 - lb) * jax.nn.sigmoid(fl))
    k = (1.0 - lb) * jax.nn.sigmoid(-fl)
    to_n = lambda t: jnp.moveaxis(_chunk(t.astype(jnp.float32)), 2, 0)
    qc, kc, vc, lfc = to_n(q), to_n(k), to_n(inp), to_n(log_f)
    mask3 = jnp.tril(jnp.ones((CHUNK, CHUNK), dtype=bool))[None, None, :, :, None]

    def step(state, xs):
        qn, kn, vn, lfn = xs
        bcum = jnp.cumsum(lfn, axis=2)
        diff = bcum[:, :, :, None, :] - bcum[:, :, None, :, :]
        dec = jnp.where(mask3, jnp.exp(jnp.where(mask3, diff, 0.0)), 0.0)
        scores = jnp.einsum('bhtk,bhsk,bhtsk->bhts', qn, kn, dec)
        o = jnp.einsum('bhts,bhsv->bhtv', scores, vn) \
            + jnp.einsum('bhtk,bhkv->bhtv', qn * jnp.exp(bcum), state)
        b_last = bcum[:, :, -1:, :]
        state = jnp.exp(b_last)[:, :, 0, :, None] * state \
            + jnp.einsum('bhsk,bhsv->bhkv', kn * jnp.exp(b_last - bcum), vn)
        return state, o

    s0 = jnp.zeros((b, h, d, inp.shape[-1]), jnp.float32)
    _, o = lax.scan(step, s0, (qc, kc, vc, lfc))
    o = _unchunk(jnp.moveaxis(o, 0, 2))
    o = _rmsnorm(o, gnorm_gain.reshape(h, -1)) * jax.nn.silu(g.astype(jnp.float32))
    return o.reshape(b, s, -1)


def _swa_sinks(q, k, v, sinks, cos, sin):
    b, s, hq, d = q.shape
    nb = s // WINDOW
    q = _apply_rope(q, cos, sin)
    k = _apply_rope(k, cos, sin)
    v = v.astype(jnp.float32)
    qb = q.reshape(b, nb, WINDOW, B_KV_HEADS, B_GROUP, d)

    def band(t):
        tp = jnp.pad(t, ((0, 0), (WINDOW, 0), (0, 0), (0, 0))).reshape(b, nb + 1, WINDOW, B_KV_HEADS, d)
        return jnp.concatenate([tp[:, :-1], tp[:, 1:]], axis=2)

    kb, vb = band(k), band(v)
    scores = jnp.einsum('bnqhgd,bnkhd->bnhgqk', qb, kb) * (d ** -0.5)
    qi = jnp.arange(WINDOW)[:, None]
    kj = jnp.arange(2 * WINDOW)[None, :]
    rel = qi + WINDOW - kj
    key_abs = jnp.arange(nb)[:, None, None] * WINDOW + kj[None] - WINDOW
    valid = ((rel >= 0) & (rel < WINDOW))[None] & (key_abs >= 0)
    scores = jnp.where(valid[None, :, None, None], scores, -jnp.inf)
    sink = jnp.broadcast_to(sinks.astype(jnp.float32).reshape(1, 1, B_KV_HEADS, B_GROUP, 1, 1),
                            scores.shape[:-1] + (1,))
    probs = jax.nn.softmax(jnp.concatenate([scores, sink], axis=-1), axis=-1)[..., :2 * WINDOW]
    o = jnp.einsum('bnhgqk,bnkhd->bnqhgd', probs, vb)
    return o.reshape(b, s, hq * d)


def _retention(q, k, v, g, cos, sin):
    b, s, h, d = q.shape
    q = _apply_rope(q, cos, sin)
    k = _apply_rope(k, cos, sin) * (d ** -0.5)
    qc, kc, vc = _chunk(q), _chunk(k), _chunk(v.astype(jnp.float32))
    log_g = jnp.log(1.0 - 2.0 ** (-5.0 - jnp.arange(h, dtype=jnp.float32)))
    idx = jnp.arange(CHUNK, dtype=jnp.float32)
    rel = idx[:, None] - idx[None, :]
    causal = rel >= 0
    dmat = jnp.where(causal[None], jnp.exp(jnp.where(causal[None], rel[None], 0.0) * log_g[:, None, None]), 0.0)
    intra = jnp.einsum('bhnid,bhnjd->bhnij', qc, kc) * dmat[None, :, None]
    o = jnp.einsum('bhnij,bhnjv->bhniv', intra, vc)
    k_dec = jnp.exp((CHUNK - 1.0 - idx)[None, :] * log_g[:, None])
    contrib = jnp.einsum('bhnjd,bhnjv->nbhdv', kc * k_dec[None, :, None, :, None], vc)
    chunk_decay = jnp.exp(CHUNK * log_g)[None, :, None, None]

    def step(state, u):
        return chunk_decay * state + u, state

    s0 = jnp.zeros((b, h, d, vc.shape[-1]), jnp.float32)
    _, s_before = lax.scan(step, s0, contrib)
    q_dec = jnp.exp((idx + 1.0)[None, :] * log_g[:, None])
    o = o + jnp.einsum('bhnid,nbhdv->bhniv', qc * q_dec[None, :, None, :, None], s_before)
    o = _unchunk(o)
    o = _rmsnorm(o) * jax.nn.silu(g.astype(jnp.float32))
    return o.reshape(b, s, -1)


def _conv_ffn(h, w_gate, w_up, conv_w, conv_b, w_down):
    gate = h @ w_gate
    gate = lax.conv_general_dilated(gate, conv_w[:, None, :].astype(gate.dtype), window_strides=(1,),
                                    padding=[(CONV_WIDTH - 1, 0)],
                                    dimension_numbers=('NWC', 'WIO', 'NWC'),
                                    feature_group_count=D_FF) + conv_b
    return (jax.nn.gelu(gate) * (h @ w_up)) @ w_down


def setup_inputs(seed: int = 0) -> dict:
    key = jax.random.key(seed)
    ks = jax.random.split(key, 20)
    nrm = lambda k, shape, scale: jax.random.normal(k, shape, jnp.float32) * scale
    x = nrm(ks[0], (BATCH, SEQ, D_MODEL), 1.0)
    p = nrm(ks[1], (DEPTH, BATCH, SEQ, PLE_DIM), 1.0)
    offset = jax.random.randint(ks[2], (BATCH, 1), 0, 4096, dtype=jnp.int32)
    positions = (offset + jnp.arange(SEQ, dtype=jnp.int32)[None, :]).astype(jnp.int32)
    return {
        "x": x,
        "p": p,
        "positions": positions,
        "attn_norm": 1.0 + nrm(ks[3], (DEPTH, D_MODEL), 0.02),
        "w_in": nrm(ks[4], (DEPTH, D_MODEL, D_IN), D_MODEL ** -0.5),
        "hgrn_lb": nrm(ks[5], (DEPTH, A_WIDTH), 0.5),
        "hgrn_gnorm": 1.0 + nrm(ks[6], (DEPTH, A_WIDTH), 0.02),
        "attn_sinks": nrm(ks[7], (DEPTH, B_HEADS), 1.0),
        "w_out": nrm(ks[8], (DEPTH, D_MIX, D_MODEL), 0.5 * D_MIX ** -0.5),
        "ffn_norm": 1.0 + nrm(ks[9], (DEPTH, D_MODEL), 0.02),
        "w_gate": nrm(ks[10], (DEPTH, D_MODEL, D_FF), D_MODEL ** -0.5),
        "w_up": nrm(ks[11], (DEPTH, D_MODEL, D_FF), D_MODEL ** -0.5),
        "conv_w": nrm(ks[12], (DEPTH, CONV_WIDTH, D_FF), CONV_WIDTH ** -0.5),
        "conv_b": nrm(ks[13], (DEPTH, D_FF), 0.02),
        "w_down": nrm(ks[14], (DEPTH, D_FF, D_MODEL), 0.5 * D_FF ** -0.5),
        "ple_norm": 1.0 + nrm(ks[15], (DEPTH, D_MODEL), 0.02),
        "w_ple_gate": nrm(ks[16], (DEPTH, D_MODEL, D_MODEL), D_MODEL ** -0.5),
        "w_ple_proj": nrm(ks[17], (DEPTH, PLE_DIM, D_MODEL), 0.5 * PLE_DIM ** -0.5),
        "final_norm": 1.0 + nrm(ks[18], (D_MODEL,), 0.02),
    }


def reference(x, p, positions, attn_norm, w_in, hgrn_lb, hgrn_gnorm, attn_sinks, w_out,
              ffn_norm, w_gate, w_up, conv_w, conv_b, w_down, ple_norm, w_ple_gate, w_ple_proj,
              final_norm):
    b, s, _ = x.shape
    dt = x.dtype
    cos, sin = _rope_tables(positions)
    lb_all = jnp.cumsum(jax.nn.softmax(hgrn_lb.astype(jnp.float32), axis=0), axis=0)
    lb_all = lb_all - lb_all[0:1]
    r = x
    for i in range(DEPTH):
        h = _rmsnorm(r, attn_norm[i]).astype(dt)
        z = h @ w_in[i]
        aq, af, ai, ag, bq, bk, bv, cq, ck, cv, cg = jnp.split(z, SPLITS, axis=-1)
        hd = lambda t: t.reshape(b, s, -1, HEAD_DIM)
        ya = _hgrn2(hd(aq), hd(af), hd(ai), hd(ag), lb_all[i], hgrn_gnorm[i])
        yb = _swa_sinks(hd(bq), hd(bk), hd(bv), attn_sinks[i], cos, sin)
        yc = _retention(hd(cq), hd(ck), hd(cv), hd(cg), cos, sin)
        mix = jnp.concatenate([ya, yb, yc], axis=-1).astype(dt)
        r = r + mix @ w_out[i]
        h2 = _rmsnorm(r, ffn_norm[i]).astype(dt)
        r = r + _conv_ffn(h2, w_gate[i], w_up[i], conv_w[i], conv_b[i], w_down[i])
        gate = jax.nn.sigmoid(_rmsnorm(r, ple_norm[i]).astype(dt) @ w_ple_gate[i])
        r = r + (p[i] @ w_ple_proj[i]) * gate
    return _rmsnorm(r, final_norm).astype(dt)
```

```python
import functools

import jax
import jax.numpy as jnp
from jax import lax
from jax.experimental import pallas as pl
from jax.experimental.pallas import tpu as pltpu

F32 = jnp.float32
BF16 = jnp.bfloat16

D_MODEL = 1024
HEAD_DIM = 64
A_WIDTH = 256
B_WIDTH = 512
B_KV_WIDTH = 128
C_WIDTH = 256
WINDOW = 128
D_IN = 2816
D_FF = 2816
PLE_DIM = 256
ROPE_THETA = 10000.0
EPS = 1e-6
NEG_BIG = -1e30

AQ, AF, AI, AG = 0, 256, 512, 768
BQ, BK, BV = 1024, 1536, 1664
CQ, CK, CV, CG = 1792, 2048, 2304, 2560

TOKEN_TILE = 512
MIX_TILE = 512
HGRN_CHUNK = 64
HGRN_SUB = 16
RET_CHUNK = 128
FF_TILE = 1408
CONV_HALO = 16
VMEM_LIMIT = 56 * 1024 * 1024


def _dot(a, b):
    return jnp.dot(a, b, preferred_element_type=F32)


def _dot_nt(a, b):
    return lax.dot_general(a, b, (((1,), (1,)), ((), ())), preferred_element_type=F32)


def _dot_tn(a, b):
    return lax.dot_general(a, b, (((0,), (0,)), ((), ())), preferred_element_type=F32)


def _iota(shape, dim):
    return lax.broadcasted_iota(jnp.int32, shape, dim)


def _sigmoid(x):
    return 1.0 / (1.0 + jnp.exp(-x))


def _rms(x, gain):
    y = x * lax.rsqrt(jnp.mean(x * x, axis=-1, keepdims=True) + EPS)
    return y * gain


def _params(*sem):
    return pltpu.CompilerParams(dimension_semantics=sem, vmem_limit_bytes=VMEM_LIMIT)


def _lower_bound_kernel(lb_ref, o_ref):
    x = lb_ref[...]
    depth = x.shape[0]
    e = jnp.exp(x - jnp.max(x, axis=0, keepdims=True))
    p = e / jnp.sum(e, axis=0, keepdims=True)
    row = _iota(x.shape, 0)
    acc = jnp.zeros_like(x)
    for j in range(depth):
        acc = acc + jnp.where(row >= j, p[j:j + 1, :], 0.0)
    o_ref[...] = acc - p[0:1, :]


def _rope_table_kernel(pos_ref, inv_ref, cos_ref, sa_ref, sb_ref):
    ang = pos_ref[...].astype(F32) * inv_ref[...]
    c = jnp.cos(ang)
    s = jnp.sin(ang)
    first_half = (_iota(ang.shape, 1) & (HEAD_DIM - 1)) < HEAD_DIM // 2
    cos_ref[...] = c
    sa_ref[...] = jnp.where(first_half, -s, 0.0)
    sb_ref[...] = jnp.where(first_half, 0.0, s)


def _inproj_kernel(x_ref, g_ref, w_ref, cos_ref, sa_ref, sb_ref, lb_ref, zt_ref, lf_ref):
    h = _rms(x_ref[...], g_ref[...]).astype(BF16)
    cos = cos_ref[...]
    sa = sa_ref[...]
    sb = sb_ref[...]

    def proj(c0, width):
        return _dot(h, w_ref[:, c0:c0 + width])

    def rope(z):
        return z * cos + pltpu.roll(z, 96, axis=1) * sa + pltpu.roll(z, 32, axis=1) * sb

    def put(c0, val):
        zt_ref[:, c0:c0 + val.shape[1]] = val.astype(BF16)

    def put_rope(c0, z, scale):
        for j in range(z.shape[1] // 128):
            put(c0 + 128 * j, rope(z[:, 128 * j:128 * (j + 1)]) * scale)

    put(AQ, proj(AQ, A_WIDTH))
    fl = proj(AF, A_WIDTH)
    e = jnp.exp(-jnp.abs(fl))
    inv = 1.0 / (1.0 + e)
    sig_pos = jnp.where(fl >= 0, inv, e * inv)
    sig_neg = jnp.where(fl >= 0, e * inv, inv)
    lb = lb_ref[...]
    lf_ref[...] = jnp.log(lb + (1.0 - lb) * sig_pos)
    put(AF, (1.0 - lb) * sig_neg)
    put(AI, proj(AI, A_WIDTH))
    g = proj(AG, A_WIDTH)
    put(AG, g * _sigmoid(g))
    put_rope(BQ, proj(BQ, B_WIDTH), HEAD_DIM ** -0.5)
    kv = proj(BK, 2 * B_KV_WIDTH)
    put_rope(BK, kv[:, :B_KV_WIDTH], 1.0)
    put(BV, kv[:, B_KV_WIDTH:])
    put_rope(CQ, proj(CQ, C_WIDTH), 1.0)
    put_rope(CK, proj(CK, C_WIDTH), HEAD_DIM ** -0.5)
    put(CV, proj(CV, C_WIDTH))
    g = proj(CG, C_WIDTH)
    put(CG, g * _sigmoid(g))


def _inproj(r, gain, w_in, cos, sa, sb, lb_all, layer):
    t = r.shape[0]
    tm = TOKEN_TILE
    row = lambda i: (i, 0)
    return pl.pallas_call(
        _inproj_kernel,
        grid=(t // tm,),
        in_specs=[
            pl.BlockSpec((tm, D_MODEL), row),
            pl.BlockSpec((None, 1, D_MODEL), lambda i: (layer, 0, 0)),
            pl.BlockSpec((None, D_MODEL, D_IN), lambda i: (layer, 0, 0)),
            pl.BlockSpec((tm, 128), row),
            pl.BlockSpec((tm, 128), row),
            pl.BlockSpec((tm, 128), row),
            pl.BlockSpec((None, 1, A_WIDTH), lambda i: (layer, 0, 0)),
        ],
        out_specs=[pl.BlockSpec((tm, D_IN), row), pl.BlockSpec((tm, A_WIDTH), row)],
        out_shape=[jax.ShapeDtypeStruct((t, D_IN), BF16), jax.ShapeDtypeStruct((t, A_WIDTH), F32)],
        compiler_params=_params("parallel"),
        name="inproj",
    )(r, gain, w_in, cos, sa, sb, lb_all)


def _hgrn_kernel(q_ref, k_ref, v_ref, sg_ref, lf_ref, gn_ref, o_ref, st_ref, *, nchunk):
    width = A_WIDTH
    cl = HGRN_CHUNK
    nsub = cl // HGRN_SUB

    @pl.when(pl.program_id(1) == 0)
    def _():
        st_ref[...] = jnp.zeros_like(st_ref)

    row = _iota((cl, width), 0)
    sub = row >> 4
    rin = row & (HGRN_SUB - 1)
    head = _iota((cl, width), 1) >> 6
    tri = (_iota((cl, cl), 0) >= _iota((cl, cl), 1)).astype(BF16)
    same_head = (_iota((width, width), 0) >> 6) == (_iota((width, width), 1) >> 6)
    sel = same_head.astype(BF16)
    gain = gn_ref[...]

    def body(c, carry):
        sl = pl.ds(pl.multiple_of(c * cl, cl), cl)
        lf = lf_ref[sl, :]
        qf = q_ref[sl, :].astype(F32)
        kf = k_ref[sl, :].astype(F32)
        vb = v_ref[sl, :]
        vf = vb.astype(F32)
        hi = lf.astype(BF16)
        rem = lf - hi.astype(F32)
        mid = rem.astype(BF16)
        lo = (rem - mid.astype(F32)).astype(BF16)
        bc = _dot(tri, hi) + _dot(tri, mid) + _dot(tri, lo)
        bl = bc[cl - 1:cl, :]
        st = st_ref[...]
        o = _dot_nt((qf * jnp.exp(bc)).astype(BF16), st.astype(BF16))
        refq = jnp.zeros_like(bc)
        ks = []
        for i in range(1, nsub):
            ri = bc[HGRN_SUB * i - 1:HGRN_SUB * i, :]
            refq = jnp.where(sub == i, ri, refq)
            ks.append(jnp.where(row < HGRN_SUB * i, kf * jnp.exp(ri - bc), 0.0).astype(BF16))
        kcat = jnp.concatenate(ks, axis=1)
        qs = qf * jnp.exp(bc - refq)
        qsub = [jnp.where(sub == i, qs, 0.0) for i in range(1, nsub)]
        for h in range(width // HEAD_DIM):
            hm = head == h
            qcat = jnp.concatenate([jnp.where(hm, q, 0.0).astype(BF16) for q in qsub], axis=1)
            sc = _dot_nt(qcat, kcat)
            o = o + _dot(sc.astype(BF16), jnp.where(hm, vb, jnp.zeros_like(vb)))
        ps = []
        for d in range(HGRN_SUB):
            if d == 0:
                p = qf * kf
            else:
                kshift = pltpu.roll(kf, d, axis=0)
                bshift = pltpu.roll(bc, d, axis=0)
                p = jnp.where(rin >= d, qf * kshift * jnp.exp(bc - bshift), 0.0)
            ps.append(p.astype(BF16))
        wsum = _dot(jnp.concatenate(ps, axis=0), sel)
        for d in range(HGRN_SUB):
            vshift = vf if d == 0 else pltpu.roll(vf, d, axis=0)
            o = o + wsum[cl * d:cl * (d + 1), :] * vshift
        kd = (kf * jnp.exp(bl - bc)).astype(BF16)
        st_ref[...] = st * jnp.exp(bl) + jnp.where(same_head, _dot_tn(vb, kd), 0.0)
        ms = _dot((o * o).astype(BF16), sel) * (1.0 / HEAD_DIM)
        y = o * lax.rsqrt(ms + EPS) * gain * sg_ref[sl, :].astype(F32)
        o_ref[sl, :] = y.astype(BF16)
        return carry

    lax.fori_loop(0, nchunk, body, 0)


def _hgrn(zt, lf, gnorm, layer):
    b, s, _ = zt.shape
    tb = MIX_TILE
    col = lambda j: pl.BlockSpec((None, tb, A_WIDTH), lambda bi, si: (bi, si, j))
    return pl.pallas_call(
        functools.partial(_hgrn_kernel, nchunk=tb // HGRN_CHUNK),
        grid=(b, s // tb),
        in_specs=[col(AQ // A_WIDTH), col(AF // A_WIDTH), col(AI // A_WIDTH), col(AG // A_WIDTH),
                  pl.BlockSpec((None, tb, A_WIDTH), lambda bi, si: (bi, si, 0)),
                  pl.BlockSpec((None, 1, A_WIDTH), lambda bi, si: (layer, 0, 0))],
        out_specs=pl.BlockSpec((None, tb, A_WIDTH), lambda bi, si: (bi, si, 0)),
        out_shape=jax.ShapeDtypeStruct((b, s, A_WIDTH), BF16),
        scratch_shapes=[pltpu.VMEM((A_WIDTH, A_WIDTH), F32)],
        compiler_params=_params("parallel", "arbitrary"),
        name="hgrn2",
    )(zt, zt, zt, zt, lf, gnorm)


def _ret_kernel(lg_ref, q_ref, k_ref, v_ref, sg_ref, o_ref, s_ref, d_ref, qd_ref, kd_ref, cd_ref, *, nchunk):
    width = C_WIDTH
    cl = RET_CHUNK
    nhead = width // HEAD_DIM
    head_row = _iota((1, width), 1) >> 6

    @pl.when(pl.program_id(1) == 0)
    def _():
        s_ref[...] = jnp.zeros_like(s_ref)
        lgl = jnp.zeros((1, width), F32)
        rel = _iota((cl, cl), 0) - _iota((cl, cl), 1)
        causal = rel >= 0
        relf = jnp.where(causal, rel, 0).astype(F32)
        for h in range(nhead):
            lgl = jnp.where(head_row == h, lg_ref[h], lgl)
            d_ref[h] = jnp.where(causal, jnp.exp(relf * lg_ref[h]), 0.0)
        t = _iota((cl, width), 0).astype(F32)
        qd_ref[...] = jnp.exp((t + 1.0) * lgl)
        kd_ref[...] = jnp.exp((cl - 1.0 - t) * lgl)
        cd_ref[...] = jnp.exp(float(cl) * lgl)

    head = _iota((cl, width), 1) >> 6
    same_head = (_iota((width, width), 0) >> 6) == (_iota((width, width), 1) >> 6)
    sel = same_head.astype(BF16)

    def body(c, carry):
        sl = pl.ds(pl.multiple_of(c * cl, cl), cl)
        q = q_ref[sl, :]
        k = k_ref[sl, :]
        v = v_ref[sl, :]
        st = s_ref[...]
        o = _dot((q.astype(F32) * qd_ref[...]).astype(BF16), st.astype(BF16))
        for h in range(nhead):
            hm = head == h
            sc = _dot_nt(jnp.where(hm, q, jnp.zeros_like(q)), k) * d_ref[h]
            o = o + _dot(sc.astype(BF16), jnp.where(hm, v, jnp.zeros_like(v)))
        kdec = (k.astype(F32) * kd_ref[...]).astype(BF16)
        s_ref[...] = st * cd_ref[...] + jnp.where(same_head, _dot_tn(kdec, v), 0.0)
        ms = _dot((o * o).astype(BF16), sel) * (1.0 / HEAD_DIM)
        y = o * lax.rsqrt(ms + EPS) * sg_ref[sl, :].astype(F32)
        o_ref[sl, :] = y.astype(BF16)
        return carry

    lax.fori_loop(0, nchunk, body, 0)


def _retention(zt, log_g):
    b, s, _ = zt.shape
    tb = MIX_TILE
    cl = RET_CHUNK
    col = lambda j: pl.BlockSpec((None, tb, C_WIDTH), lambda bi, si: (bi, si, j))
    return pl.pallas_call(
        functools.partial(_ret_kernel, nchunk=tb // cl),
        grid=(b, s // tb),
        in_specs=[pl.BlockSpec(memory_space=pltpu.SMEM),
                  col(CQ // C_WIDTH), col(CK // C_WIDTH), col(CV // C_WIDTH), col(CG // C_WIDTH)],
        out_specs=pl.BlockSpec((None, tb, C_WIDTH), lambda bi, si: (bi, si, 0)),
        out_shape=jax.ShapeDtypeStruct((b, s, C_WIDTH), BF16),
        scratch_shapes=[pltpu.VMEM((C_WIDTH, C_WIDTH), F32),
                        pltpu.VMEM((C_WIDTH // HEAD_DIM, cl, cl), F32),
                        pltpu.VMEM((cl, C_WIDTH), F32),
                        pltpu.VMEM((cl, C_WIDTH), F32),
                        pltpu.VMEM((1, C_WIDTH), F32)],
        compiler_params=_params("parallel", "arbitrary"),
        name="retention",
    )(log_g, zt, zt, zt, zt)


def _swa_kernel(sink_ref, q_ref, k_ref, v_ref, kp_ref, vp_ref, o_ref, *, nblk):
    w = WINDOW
    first_tile = pl.program_id(1) == 0
    low = _iota((2 * w, 128), 1) < HEAD_DIM
    qi = _iota((w, 2 * w), 0)
    kj = _iota((w, 2 * w), 1)
    rel = qi + w - kj
    band = (rel >= 0) & (rel < w)
    band_first = band & ((kj >= w) | jnp.logical_not(first_tile))
    ones = jnp.ones((2 * w, 128), BF16)
    zero = jnp.zeros((2 * w, 128), BF16)

    for blk in range(nblk):
        cur = slice(blk * w, (blk + 1) * w)
        if blk == 0:
            kprev, vprev, valid = kp_ref[...], vp_ref[...], band_first
        else:
            prev = slice((blk - 1) * w, blk * w)
            kprev, vprev, valid = k_ref[prev, :], v_ref[prev, :], band
        kk = jnp.concatenate([kprev, k_ref[cur, :]], axis=0)
        vv = jnp.concatenate([vprev, v_ref[cur, :]], axis=0)
        kr = pltpu.roll(kk.astype(F32), HEAD_DIM, axis=1).astype(BF16)
        vr = pltpu.roll(vv.astype(F32), HEAD_DIM, axis=1).astype(BF16)
        kop = {}
        vop = {}
        for g in range(2):
            for half in range(2):
                keep = low if half == 0 else jnp.logical_not(low)
                kop[g, half] = jnp.where(keep, kk if g == half else kr, zero)
                vop[g, half] = jnp.concatenate([jnp.where(keep, vv if g == half else vr, zero), ones], axis=1)
        for j in range(B_WIDTH // 128):
            qcol = q_ref[cur, 128 * j:128 * (j + 1)]
            acc = jnp.zeros((w, 128), F32)
            for half in range(2):
                h = 2 * j + half
                g = h // 4
                sink = sink_ref[h]
                s = jnp.where(valid, _dot_nt(qcol, kop[g, half]), NEG_BIG)
                m = jnp.maximum(jnp.max(s, axis=-1, keepdims=True), sink)
                p = jnp.exp(s - m).astype(BF16)
                ol = _dot(p, vop[g, half])
                acc = acc + ol[:, :128] / (ol[:, 128:] + jnp.exp(sink - m))
            o_ref[cur, 128 * j:128 * (j + 1)] = acc.astype(BF16)


def _swa(zt, sinks):
    b, s, _ = zt.shape
    tb = MIX_TILE
    per = tb // WINDOW
    prev = lambda j: pl.BlockSpec((None, WINDOW, B_KV_WIDTH),
                                  lambda bi, si: (bi, jnp.maximum(si * per - 1, 0), j))
    return pl.pallas_call(
        functools.partial(_swa_kernel, nblk=per),
        grid=(b, s // tb),
        in_specs=[pl.BlockSpec(memory_space=pltpu.SMEM),
                  pl.BlockSpec((None, tb, B_WIDTH), lambda bi, si: (bi, si, BQ // B_WIDTH)),
                  pl.BlockSpec((None, tb, B_KV_WIDTH), lambda bi, si: (bi, si, BK // B_KV_WIDTH)),
                  pl.BlockSpec((None, tb, B_KV_WIDTH), lambda bi, si: (bi, si, BV // B_KV_WIDTH)),
                  prev(BK // B_KV_WIDTH), prev(BV // B_KV_WIDTH)],
        out_specs=pl.BlockSpec((None, tb, B_WIDTH), lambda bi, si: (bi, si, 0)),
        out_shape=jax.ShapeDtypeStruct((b, s, B_WIDTH), BF16),
        compiler_params=_params("parallel", "arbitrary"),
        name="swa",
    )(sinks, zt, zt, zt, zt, zt)


def _outproj_kernel(r_ref, ya_ref, yb_ref, yc_ref, w_ref, g_ref, ro_ref, h_ref):
    b0 = A_WIDTH
    c0 = A_WIDTH + B_WIDTH
    acc = r_ref[...] + _dot(ya_ref[...], w_ref[0:b0, :])
    acc = acc + _dot(yb_ref[...], w_ref[b0:c0, :])
    acc = acc + _dot(yc_ref[...], w_ref[c0:, :])
    ro_ref[...] = acc
    h_ref[...] = _rms(acc, g_ref[...]).astype(BF16)


def _outproj(r, ya, yb, yc, w_out, gain, layer):
    t = r.shape[0]
    tm = TOKEN_TILE
    row = lambda i: (i, 0)
    return pl.pallas_call(
        _outproj_kernel,
        grid=(t // tm,),
        in_specs=[pl.BlockSpec((tm, D_MODEL), row),
                  pl.BlockSpec((tm, A_WIDTH), row),
                  pl.BlockSpec((tm, B_WIDTH), row),
                  pl.BlockSpec((tm, C_WIDTH), row),
                  pl.BlockSpec((None, D_MODEL, D_MODEL), lambda i: (layer, 0, 0)),
                  pl.BlockSpec((None, 1, D_MODEL), lambda i: (layer, 0, 0))],
        out_specs=[pl.BlockSpec((tm, D_MODEL), row), pl.BlockSpec((tm, D_MODEL), row)],
        out_shape=[jax.ShapeDtypeStruct((t, D_MODEL), F32), jax.ShapeDtypeStruct((t, D_MODEL), BF16)],
        compiler_params=_params("parallel"),
        name="outproj",
    )(r, ya, yb, yc, w_out, gain)


def _ffn_kernel(h_ref, hp_ref, r_ref, wg_ref, wu_ref, cw_ref, cb_ref, wd_ref, o_ref, lhs_ref, ge_ref,
                *, tiles_per_seq):
    tm = h_ref.shape[0]
    f = pl.program_id(1)

    @pl.when(f == 0)
    def _():
        seq_start = pl.program_id(0) % tiles_per_seq == 0
        halo = hp_ref[...]
        lhs_ref[0:CONV_HALO, :] = jnp.where(seq_start, jnp.zeros_like(halo), halo)
        lhs_ref[CONV_HALO:, :] = h_ref[...]
        o_ref[...] = r_ref[...]

    ge_ref[...] = _dot(lhs_ref[...], wg_ref[...])
    cw = cw_ref[...]
    gate = (ge_ref[pl.ds(CONV_HALO - 2, tm), :] * cw[0:1, :]
            + ge_ref[pl.ds(CONV_HALO - 1, tm), :] * cw[1:2, :]
            + ge_ref[pl.ds(CONV_HALO, tm), :] * cw[2:3, :]
            + cb_ref[...])
    up = _dot(h_ref[...], wu_ref[...])
    act = (jax.nn.gelu(gate, approximate=True) * up).astype(BF16)
    o_ref[...] += _dot(act, wd_ref[...])


def _ffn(h, r, w_gate, w_up, conv_w, conv_b, w_down, layer, seq_len):
    t = r.shape[0]
    tm = TOKEN_TILE
    tf = FF_TILE
    per = tm // CONV_HALO
    return pl.pallas_call(
        functools.partial(_ffn_kernel, tiles_per_seq=seq_len // tm),
        grid=(t // tm, D_FF // tf),
        in_specs=[pl.BlockSpec((tm, D_MODEL), lambda i, f: (i, 0)),
                  pl.BlockSpec((CONV_HALO, D_MODEL), lambda i, f: (jnp.maximum(i * per - 1, 0), 0)),
                  pl.BlockSpec((tm, D_MODEL), lambda i, f: (i, 0)),
                  pl.BlockSpec((None, D_MODEL, tf), lambda i, f: (layer, 0, f)),
                  pl.BlockSpec((None, D_MODEL, tf), lambda i, f: (layer, 0, f)),
                  pl.BlockSpec((None, 3, tf), lambda i, f: (layer, 0, f)),
                  pl.BlockSpec((None, 1, tf), lambda i, f: (layer, 0, f)),
                  pl.BlockSpec((None, tf, D_MODEL), lambda i, f: (layer, f, 0))],
        out_specs=pl.BlockSpec((tm, D_MODEL), lambda i, f: (i, 0)),
        out_shape=jax.ShapeDtypeStruct((t, D_MODEL), F32),
        scratch_shapes=[pltpu.VMEM((tm + CONV_HALO, D_MODEL), BF16),
                        pltpu.VMEM((tm + CONV_HALO, tf), F32)],
        compiler_params=_params("parallel", "arbitrary"),
        name="conv_mlp",
    )(h, h, r, w_gate, w_up, conv_w, conv_b, w_down)


def _ple_kernel(r_ref, p_ref, g_ref, wg_ref, wp_ref, o_ref):
    r = r_ref[...]
    gate = _sigmoid(_dot(_rms(r, g_ref[...]).astype(BF16), wg_ref[...]))
    emb = _dot(p_ref[...].astype(BF16), wp_ref[...])
    o_ref[...] = r + emb * gate


def _ple(r, p, gain, w_gate, w_proj, layer):
    t = r.shape[0]
    tm = TOKEN_TILE
    row = lambda i: (i, 0)
    return pl.pallas_call(
        _ple_kernel,
        grid=(t // tm,),
        in_specs=[pl.BlockSpec((tm, D_MODEL), row),
                  pl.BlockSpec((None, tm, PLE_DIM), lambda i: (layer, i, 0)),
                  pl.BlockSpec((None, 1, D_MODEL), lambda i: (layer, 0, 0)),
                  pl.BlockSpec((None, D_MODEL, D_MODEL), lambda i: (layer, 0, 0)),
                  pl.BlockSpec((None, PLE_DIM, D_MODEL), lambda i: (layer, 0, 0))],
        out_specs=pl.BlockSpec((tm, D_MODEL), row),
        out_shape=jax.ShapeDtypeStruct((t, D_MODEL), F32),
        compiler_params=_params("parallel"),
        name="ple",
    )(r, p, gain, w_gate, w_proj)


def _final_norm_kernel(r_ref, g_ref, o_ref):
    o_ref[...] = _rms(r_ref[...], g_ref[...])


def _final_norm(r, gain):
    t = r.shape[0]
    tm = TOKEN_TILE
    return pl.pallas_call(
        _final_norm_kernel,
        grid=(t // tm,),
        in_specs=[pl.BlockSpec((tm, D_MODEL), lambda i: (i, 0)), pl.BlockSpec((1, D_MODEL), lambda i: (0, 0))],
        out_specs=pl.BlockSpec((tm, D_MODEL), lambda i: (i, 0)),
        out_shape=jax.ShapeDtypeStruct((t, D_MODEL), F32),
        compiler_params=_params("parallel"),
        name="final_norm",
    )(r, gain)


def kernel(x, p, positions, attn_norm, w_in, hgrn_lb, hgrn_gnorm, attn_sinks, w_out, ffn_norm, w_gate, w_up,
           conv_w, conv_b, w_down, ple_norm, w_ple_gate, w_ple_proj, final_norm):
    b, s, d = x.shape
    depth = w_in.shape[0]
    t = b * s
    assert d == D_MODEL and s % MIX_TILE == 0 and s % TOKEN_TILE == 0 and MIX_TILE % RET_CHUNK == 0

    vec = lambda a: a.reshape(depth, 1, a.shape[-1])
    bf = lambda a: a.astype(BF16)

    lb_all = pl.pallas_call(
        _lower_bound_kernel, out_shape=jax.ShapeDtypeStruct(hgrn_lb.shape, F32), name="hgrn_lower_bound",
    )(hgrn_lb.astype(F32))

    inv = 1.0 / (ROPE_THETA ** (jnp.arange(0, HEAD_DIM, 2, dtype=F32) / HEAD_DIM))
    inv = jnp.tile(inv, 4).reshape(1, 128)
    tt = TOKEN_TILE
    cos, sin_a, sin_b = pl.pallas_call(
        _rope_table_kernel,
        grid=(t // tt,),
        in_specs=[pl.BlockSpec((tt, 1), lambda i: (i, 0)), pl.BlockSpec((1, 128), lambda i: (0, 0))],
        out_specs=[pl.BlockSpec((tt, 128), lambda i: (i, 0))] * 3,
        out_shape=[jax.ShapeDtypeStruct((t, 128), F32)] * 3,
        compiler_params=_params("parallel"),
        name="rope_tables",
    )(positions.reshape(t, 1), inv)

    nhead_c = C_WIDTH // HEAD_DIM
    log_g = jnp.log(1.0 - 2.0 ** (-5.0 - jnp.arange(nhead_c, dtype=F32)))

    w_in_b, w_out_b, w_gate_b, w_up_b, w_down_b = bf(w_in), bf(w_out), bf(w_gate), bf(w_up), bf(w_down)
    w_pg_b, w_pp_b = bf(w_ple_gate), bf(w_ple_proj)
    attn_norm_v, ffn_norm_v, ple_norm_v = vec(attn_norm), vec(ffn_norm), vec(ple_norm)
    gnorm_v, conv_b_v, lb_v = vec(hgrn_gnorm), vec(conv_b), vec(lb_all)
    p_flat = p.reshape(depth, t, PLE_DIM)

    r = x.reshape(t, d).astype(F32)
    for i in range(depth):
        zt, lf = _inproj(r, attn_norm_v, w_in_b, cos, sin_a, sin_b, lb_v, i)
        zt3 = zt.reshape(b, s, D_IN)
        ya = _hgrn(zt3, lf.reshape(b, s, A_WIDTH), gnorm_v, i)
        yb = _swa(zt3, attn_sinks[i].astype(F32))
        yc = _retention(zt3, log_g)
        r, h2 = _outproj(r, ya.reshape(t, A_WIDTH), yb.reshape(t, B_WIDTH), yc.reshape(t, C_WIDTH),
                         w_out_b, ffn_norm_v, i)
        r = _ffn(h2, r, w_gate_b, w_up_b, conv_w, conv_b_v, w_down_b, i, s)
        r = _ple(r, p_flat, ple_norm_v, w_pg_b, w_pp_b, i)
    out = _final_norm(r, final_norm.reshape(1, d))
    return out.reshape(b, s, d).astype(x.dtype)
```

```python
import functools

import jax
import jax.numpy as jnp
from jax import lax
from jax.experimental import pallas as pl
from jax.experimental.pallas import tpu as pltpu

F32 = jnp.float32
BF16 = jnp.bfloat16

D_MODEL = 1024
HEAD_DIM = 64
A_WIDTH = 256
B_WIDTH = 512
B_KV_WIDTH = 128
C_WIDTH = 256
WINDOW = 128
D_IN = 2816
D_FF = 2816
PLE_DIM = 256
ROPE_THETA = 10000.0
EPS = 1e-6
NEG_BIG = -1e30

AQ, AF, AI, AG = 0, 256, 512, 768
BQ, BK, BV = 1024, 1536, 1664
CQ, CK, CV, CG = 1792, 2048, 2304, 2560

TOKEN_TILE = 512
MIX_TILE = 512
HGRN_CHUNK = 64
RET_CHUNK = 128
FF_TILE = 1408
CONV_HALO = 16
VMEM_LIMIT = 56 * 1024 * 1024


def _dot(a, b):
    return jnp.dot(a, b, preferred_element_type=F32)


def _dot_nt(a, b):
    return lax.dot_general(a, b, (((1,), (1,)), ((), ())), preferred_element_type=F32)


def _dot_tn(a, b):
    return lax.dot_general(a, b, (((0,), (0,)), ((), ())), preferred_element_type=F32)


def _iota(shape, dim):
    return lax.broadcasted_iota(jnp.int32, shape, dim)


def _sigmoid(x):
    return 1.0 / (1.0 + jnp.exp(-x))


def _rms(x, gain):
    y = x * lax.rsqrt(jnp.mean(x * x, axis=-1, keepdims=True) + EPS)
    return y * gain


def _params(*sem):
    return pltpu.CompilerParams(dimension_semantics=sem, vmem_limit_bytes=VMEM_LIMIT)


def _lower_bound_kernel(lb_ref, o_ref):
    x = lb_ref[...]
    depth = x.shape[0]
    e = jnp.exp(x - jnp.max(x, axis=0, keepdims=True))
    p = e / jnp.sum(e, axis=0, keepdims=True)
    row = _iota(x.shape, 0)
    acc = jnp.zeros_like(x)
    for j in range(depth):
        acc = acc + jnp.where(row >= j, p[j:j + 1, :], 0.0)
    o_ref[...] = acc - p[0:1, :]


def _rope_table_kernel(pos_ref, inv_ref, cos_ref, sa_ref, sb_ref):
    ang = pos_ref[...].astype(F32) * inv_ref[...]
    c = jnp.cos(ang)
    s = jnp.sin(ang)
    first_half = (_iota(ang.shape, 1) & (HEAD_DIM - 1)) < HEAD_DIM // 2
    cos_ref[...] = c
    sa_ref[...] = jnp.where(first_half, -s, 0.0)
    sb_ref[...] = jnp.where(first_half, 0.0, s)


def _inproj_kernel(x_ref, g_ref, w_ref, cos_ref, sa_ref, sb_ref, lb_ref, zt_ref, lf_ref):
    h = _rms(x_ref[...], g_ref[...]).astype(BF16)
    cos = cos_ref[...]
    sa = sa_ref[...]
    sb = sb_ref[...]

    def proj(c0, width):
        return _dot(h, w_ref[:, c0:c0 + width])

    def rope(z):
        return z * cos + pltpu.roll(z, 96, axis=1) * sa + pltpu.roll(z, 32, axis=1) * sb

    def put(c0, val):
        zt_ref[:, c0:c0 + val.shape[1]] = val.astype(BF16)

    def put_rope(c0, z, scale):
        for j in range(z.shape[1] // 128):
            put(c0 + 128 * j, rope(z[:, 128 * j:128 * (j + 1)]) * scale)

    put(AQ, proj(AQ, A_WIDTH))
    fl = proj(AF, A_WIDTH)
    e = jnp.exp(-jnp.abs(fl))
    inv = 1.0 / (1.0 + e)
    sig_pos = jnp.where(fl >= 0, inv, e * inv)
    sig_neg = jnp.where(fl >= 0, e * inv, inv)
    lb = lb_ref[...]
    lf_ref[...] = jnp.log(lb + (1.0 - lb) * sig_pos)
    put(AF, (1.0 - lb) * sig_neg)
    put(AI, proj(AI, A_WIDTH))
    g = proj(AG, A_WIDTH)
    put(AG, g * _sigmoid(g))
    put_rope(BQ, proj(BQ, B_WIDTH), HEAD_DIM ** -0.5)
    kv = proj(BK, 2 * B_KV_WIDTH)
    put_rope(BK, kv[:, :B_KV_WIDTH], 1.0)
    put(BV, kv[:, B_KV_WIDTH:])
    put_rope(CQ, proj(CQ, C_WIDTH), 1.0)
    put_rope(CK, proj(CK, C_WIDTH), HEAD_DIM ** -0.5)
    put(CV, proj(CV, C_WIDTH))
    g = proj(CG, C_WIDTH)
    put(CG, g * _sigmoid(g))


def _inproj(r, gain, w_in, cos, sa, sb, lb_all, layer):
    t = r.shape[0]
    tm = TOKEN_TILE
    row = lambda i: (i, 0)
    return pl.pallas_call(
        _inproj_kernel,
        grid=(t // tm,),
        in_specs=[
            pl.BlockSpec((tm, D_MODEL), row),
            pl.BlockSpec((None, 1, D_MODEL), lambda i: (layer, 0, 0)),
            pl.BlockSpec((None, D_MODEL, D_IN), lambda i: (layer, 0, 0)),
            pl.BlockSpec((tm, 128), row),
            pl.BlockSpec((tm, 128), row),
            pl.BlockSpec((tm, 128), row),
            pl.BlockSpec((None, 1, A_WIDTH), lambda i: (layer, 0, 0)),
        ],
        out_specs=[pl.BlockSpec((tm, D_IN), row), pl.BlockSpec((tm, A_WIDTH), row)],
        out_shape=[jax.ShapeDtypeStruct((t, D_IN), BF16), jax.ShapeDtypeStruct((t, A_WIDTH), F32)],
        compiler_params=_params("parallel"),
        name="inproj",
    )(r, gain, w_in, cos, sa, sb, lb_all)


def _hgrn_kernel(q_ref, k_ref, v_ref, sg_ref, lf_ref, gn_ref, o_ref, st_ref, m3_ref, lev_ref, *, nchunk):
    width = A_WIDTH
    cl = HGRN_CHUNK
    nlev = cl.bit_length() - 1
    nhead = width // HEAD_DIM

    @pl.when(pl.program_id(1) == 0)
    def _():
        st_ref[...] = jnp.zeros_like(st_ref)
        t = _iota((cl, 3 * cl), 0)
        s = _iota((cl, 3 * cl), 1) & (cl - 1)
        m3_ref[0:cl, :] = (s <= t).astype(BF16)
        for l in range(nlev):
            m = 1 << l
            ref_row = (t & -(2 * m)) + (m - 1)
            m3_ref[cl * (l + 1):cl * (l + 2), :] = (s <= ref_row).astype(BF16)
        tq = _iota((cl, width), 0)
        sk = _iota((cl, width), 1) & (cl - 1)
        diff = tq ^ sk
        lvl = jnp.zeros((cl, width), jnp.int32)
        for j in range(1, nlev):
            lvl = lvl + (diff >= (1 << j)).astype(jnp.int32)
        lev_ref[...] = jnp.where(tq > sk, lvl, jnp.where(tq == sk, -2, -1))

    row = _iota((cl, width), 0)
    head = _iota((cl, width), 1) >> 6
    hmask = [(head == h).astype(BF16) for h in range(nhead)]
    same_head = (_iota((width, width), 0) >> 6) == (_iota((width, width), 1) >> 6)
    sel = same_head.astype(BF16)
    gain = gn_ref[...]

    def by_head(x):
        return jnp.concatenate([x * hmask[h] for h in range(nhead)], axis=0)

    def body(c, carry):
        sl = pl.ds(pl.multiple_of(c * cl, cl), cl)
        lf = lf_ref[sl, :]
        qf = q_ref[sl, :].astype(F32)
        kf = k_ref[sl, :].astype(F32)
        vb = v_ref[sl, :]
        hi = lf.astype(BF16)
        rem = lf - hi.astype(F32)
        mid = rem.astype(BF16)
        lo = (rem - mid.astype(F32)).astype(BF16)
        sums = _dot(m3_ref[...], jnp.concatenate([hi, mid, lo], axis=0))
        bc = sums[0:cl]
        bl = bc[cl - 1:cl, :]
        st = st_ref[...]
        o = _dot_nt((qf * jnp.exp(bc)).astype(BF16), st.astype(BF16))
        lev = lev_ref[...]
        sc = jnp.zeros((cl, width), F32)
        for l in range(nlev):
            ref = sums[cl * (l + 1):cl * (l + 2)]
            e = jnp.exp(-jnp.abs(bc - ref))
            x = (jnp.where((row & (1 << l)) != 0, qf, kf) * e).astype(BF16)
            sc = jnp.where(lev == l, _dot_nt(x, by_head(x)), sc)
        sc = jnp.where(lev == -2, _dot((qf * kf).astype(BF16), sel), sc)
        o = o + _dot(sc.astype(BF16), by_head(vb))
        kd = (kf * jnp.exp(bl - bc)).astype(BF16)
        st_ref[...] = st * jnp.exp(bl) + jnp.where(same_head, _dot_tn(vb, kd), 0.0)
        ms = _dot((o * o).astype(BF16), sel) * (1.0 / HEAD_DIM)
        y = o * lax.rsqrt(ms + EPS) * gain * sg_ref[sl, :].astype(F32)
        o_ref[sl, :] = y.astype(BF16)
        return carry

    lax.fori_loop(0, nchunk, body, 0, unroll=True)


def _hgrn(zt, lf, gnorm, layer):
    b, s, _ = zt.shape
    tb = MIX_TILE
    col = lambda j: pl.BlockSpec((None, tb, A_WIDTH), lambda bi, si: (bi, si, j))
    return pl.pallas_call(
        functools.partial(_hgrn_kernel, nchunk=tb // HGRN_CHUNK),
        grid=(b, s // tb),
        in_specs=[col(AQ // A_WIDTH), col(AF // A_WIDTH), col(AI // A_WIDTH), col(AG // A_WIDTH),
                  pl.BlockSpec((None, tb, A_WIDTH), lambda bi, si: (bi, si, 0)),
                  pl.BlockSpec((None, 1, A_WIDTH), lambda bi, si: (layer, 0, 0))],
        out_specs=pl.BlockSpec((None, tb, A_WIDTH), lambda bi, si: (bi, si, 0)),
        out_shape=jax.ShapeDtypeStruct((b, s, A_WIDTH), BF16),
        scratch_shapes=[pltpu.VMEM((A_WIDTH, A_WIDTH), F32),
                        pltpu.VMEM((HGRN_CHUNK * HGRN_CHUNK.bit_length(), 3 * HGRN_CHUNK), BF16),
                        pltpu.VMEM((HGRN_CHUNK, A_WIDTH), jnp.int32)],
        compiler_params=_params("parallel", "arbitrary"),
        name="hgrn2",
    )(zt, zt, zt, zt, lf, gnorm)


def _ret_kernel(lg_ref, q_ref, k_ref, v_ref, sg_ref, o_ref, s_ref, d_ref, qd_ref, kd_ref, cd_ref, *, nchunk):
    width = C_WIDTH
    cl = RET_CHUNK
    nhead = width // HEAD_DIM
    head_row = _iota((1, width), 1) >> 6

    @pl.when(pl.program_id(1) == 0)
    def _():
        s_ref[...] = jnp.zeros_like(s_ref)
        lgl = jnp.zeros((1, width), F32)
        rel = _iota((cl, cl), 0) - _iota((cl, cl), 1)
        causal = rel >= 0
        relf = jnp.where(causal, rel, 0).astype(F32)
        for h in range(nhead):
            lgl = jnp.where(head_row == h, lg_ref[h], lgl)
            d_ref[:, cl * h:cl * (h + 1)] = jnp.where(causal, jnp.exp(relf * lg_ref[h]), 0.0)
        t = _iota((cl, width), 0).astype(F32)
        qd_ref[...] = jnp.exp((t + 1.0) * lgl)
        kd_ref[...] = jnp.exp((cl - 1.0 - t) * lgl)
        cd_ref[...] = jnp.exp(float(cl) * lgl)

    head = _iota((cl, width), 1) >> 6
    hmask = [(head == h).astype(BF16) for h in range(nhead)]
    same_head = (_iota((width, width), 0) >> 6) == (_iota((width, width), 1) >> 6)
    sel = same_head.astype(BF16)

    def by_head(x):
        return jnp.concatenate([x * hmask[h] for h in range(nhead)], axis=0)

    def body(c, carry):
        sl = pl.ds(pl.multiple_of(c * cl, cl), cl)
        q = q_ref[sl, :]
        k = k_ref[sl, :]
        v = v_ref[sl, :]
        st = s_ref[...]
        o = _dot((q.astype(F32) * qd_ref[...]).astype(BF16), st.astype(BF16))
        sc = _dot_nt(q, by_head(k)) * d_ref[...]
        o = o + _dot(sc.astype(BF16), by_head(v))
        kdec = (k.astype(F32) * kd_ref[...]).astype(BF16)
        s_ref[...] = st * cd_ref[...] + jnp.where(same_head, _dot_tn(kdec, v), 0.0)
        ms = _dot((o * o).astype(BF16), sel) * (1.0 / HEAD_DIM)
        y = o * lax.rsqrt(ms + EPS) * sg_ref[sl, :].astype(F32)
        o_ref[sl, :] = y.astype(BF16)
        return carry

    lax.fori_loop(0, nchunk, body, 0, unroll=2)


def _retention(zt, log_g):
    b, s, _ = zt.shape
    tb = MIX_TILE
    cl = RET_CHUNK
    col = lambda j: pl.BlockSpec((None, tb, C_WIDTH), lambda bi, si: (bi, si, j))
    return pl.pallas_call(
        functools.partial(_ret_kernel, nchunk=tb // cl),
        grid=(b, s // tb),
        in_specs=[pl.BlockSpec(memory_space=pltpu.SMEM),
                  col(CQ // C_WIDTH), col(CK // C_WIDTH), col(CV // C_WIDTH), col(CG // C_WIDTH)],
        out_specs=pl.BlockSpec((None, tb, C_WIDTH), lambda bi, si: (bi, si, 0)),
        out_shape=jax.ShapeDtypeStruct((b, s, C_WIDTH), BF16),
        scratch_shapes=[pltpu.VMEM((C_WIDTH, C_WIDTH), F32),
                        pltpu.VMEM((cl, C_WIDTH // HEAD_DIM * cl), F32),
                        pltpu.VMEM((cl, C_WIDTH), F32),
                        pltpu.VMEM((cl, C_WIDTH), F32),
                        pltpu.VMEM((1, C_WIDTH), F32)],
        compiler_params=_params("parallel", "arbitrary"),
        name="retention",
    )(log_g, zt, zt, zt, zt)


def _swa_kernel(sink_ref, q_ref, k_ref, v_ref, kp_ref, vp_ref, o_ref, *, nblk):
    w = WINDOW
    first_tile = pl.program_id(1) == 0
    low = _iota((2 * w, 128), 1) < HEAD_DIM
    qi = _iota((w, 2 * w), 0)
    kj = _iota((w, 2 * w), 1)
    rel = qi + w - kj
    band = (rel >= 0) & (rel < w)
    band_first = band & ((kj >= w) | jnp.logical_not(first_tile))
    ones = jnp.ones((2 * w, 128), BF16)
    zero = jnp.zeros((2 * w, 128), BF16)

    for blk in range(nblk):
        cur = slice(blk * w, (blk + 1) * w)
        if blk == 0:
            kprev, vprev, valid = kp_ref[...], vp_ref[...], band_first
        else:
            prev = slice((blk - 1) * w, blk * w)
            kprev, vprev, valid = k_ref[prev, :], v_ref[prev, :], band
        kk = jnp.concatenate([kprev, k_ref[cur, :]], axis=0)
        vv = jnp.concatenate([vprev, v_ref[cur, :]], axis=0)
        kr = pltpu.roll(kk.astype(F32), HEAD_DIM, axis=1).astype(BF16)
        vr = pltpu.roll(vv.astype(F32), HEAD_DIM, axis=1).astype(BF16)
        kop = {}
        vop = {}
        for g in range(2):
            for half in range(2):
                keep = low if half == 0 else jnp.logical_not(low)
                kop[g, half] = jnp.where(keep, kk if g == half else kr, zero)
                vop[g, half] = jnp.concatenate([jnp.where(keep, vv if g == half else vr, zero), ones], axis=1)
        for j in range(B_WIDTH // 128):
            qcol = q_ref[cur, 128 * j:128 * (j + 1)]
            acc = jnp.zeros((w, 128), F32)
            for half in range(2):
                h = 2 * j + half
                g = h // 4
                sink = sink_ref[h]
                s = jnp.where(valid, _dot_nt(qcol, kop[g, half]), NEG_BIG)
                m = jnp.maximum(jnp.max(s, axis=-1, keepdims=True), sink)
                p = jnp.exp(s - m).astype(BF16)
                ol = _dot(p, vop[g, half])
                acc = acc + ol[:, :128] / (ol[:, 128:] + jnp.exp(sink - m))
            o_ref[cur, 128 * j:128 * (j + 1)] = acc.astype(BF16)


def _swa(zt, sinks):
    b, s, _ = zt.shape
    tb = MIX_TILE
    per = tb // WINDOW
    prev = lambda j: pl.BlockSpec((None, WINDOW, B_KV_WIDTH),
                                  lambda bi, si: (bi, jnp.maximum(si * per - 1, 0), j))
    return pl.pallas_call(
        functools.partial(_swa_kernel, nblk=per),
        grid=(b, s // tb),
        in_specs=[pl.BlockSpec(memory_space=pltpu.SMEM),
                  pl.BlockSpec((None, tb, B_WIDTH), lambda bi, si: (bi, si, BQ // B_WIDTH)),
                  pl.BlockSpec((None, tb, B_KV_WIDTH), lambda bi, si: (bi, si, BK // B_KV_WIDTH)),
                  pl.BlockSpec((None, tb, B_KV_WIDTH), lambda bi, si: (bi, si, BV // B_KV_WIDTH)),
                  prev(BK // B_KV_WIDTH), prev(BV // B_KV_WIDTH)],
        out_specs=pl.BlockSpec((None, tb, B_WIDTH), lambda bi, si: (bi, si, 0)),
        out_shape=jax.ShapeDtypeStruct((b, s, B_WIDTH), BF16),
        compiler_params=_params("parallel", "arbitrary"),
        name="swa",
    )(sinks, zt, zt, zt, zt, zt)


def _outproj_kernel(r_ref, ya_ref, yb_ref, yc_ref, w_ref, g_ref, ro_ref, h_ref):
    b0 = A_WIDTH
    c0 = A_WIDTH + B_WIDTH
    acc = r_ref[...] + _dot(ya_ref[...], w_ref[0:b0, :])
    acc = acc + _dot(yb_ref[...], w_ref[b0:c0, :])
    acc = acc + _dot(yc_ref[...], w_ref[c0:, :])
    ro_ref[...] = acc
    h_ref[...] = _rms(acc, g_ref[...]).astype(BF16)


def _outproj(r, ya, yb, yc, w_out, gain, layer):
    t = r.shape[0]
    tm = TOKEN_TILE
    row = lambda i: (i, 0)
    return pl.pallas_call(
        _outproj_kernel,
        grid=(t // tm,),
        in_specs=[pl.BlockSpec((tm, D_MODEL), row),
                  pl.BlockSpec((tm, A_WIDTH), row),
                  pl.BlockSpec((tm, B_WIDTH), row),
                  pl.BlockSpec((tm, C_WIDTH), row),
                  pl.BlockSpec((None, D_MODEL, D_MODEL), lambda i: (layer, 0, 0)),
                  pl.BlockSpec((None, 1, D_MODEL), lambda i: (layer, 0, 0))],
        out_specs=[pl.BlockSpec((tm, D_MODEL), row), pl.BlockSpec((tm, D_MODEL), row)],
        out_shape=[jax.ShapeDtypeStruct((t, D_MODEL), F32), jax.ShapeDtypeStruct((t, D_MODEL), BF16)],
        compiler_params=_params("parallel"),
        name="outproj",
    )(r, ya, yb, yc, w_out, gain)


def _ffn_kernel(h_ref, hp_ref, r_ref, wg_ref, wu_ref, cw_ref, cb_ref, wd_ref, o_ref, lhs_ref, ge_ref,
                *, tiles_per_seq):
    tm = h_ref.shape[0]
    f = pl.program_id(1)

    @pl.when(f == 0)
    def _():
        seq_start = pl.program_id(0) % tiles_per_seq == 0
        halo = hp_ref[...]
        lhs_ref[0:CONV_HALO, :] = jnp.where(seq_start, jnp.zeros_like(halo), halo)
        lhs_ref[CONV_HALO:, :] = h_ref[...]
        o_ref[...] = r_ref[...]

    ge_ref[...] = _dot(lhs_ref[...], wg_ref[...])
    cw = cw_ref[...]
    gate = (ge_ref[pl.ds(CONV_HALO - 2, tm), :] * cw[0:1, :]
            + ge_ref[pl.ds(CONV_HALO - 1, tm), :] * cw[1:2, :]
            + ge_ref[pl.ds(CONV_HALO, tm), :] * cw[2:3, :]
            + cb_ref[...])
    up = _dot(h_ref[...], wu_ref[...])
    act = (jax.nn.gelu(gate, approximate=True) * up).astype(BF16)
    o_ref[...] += _dot(act, wd_ref[...])


def _ffn(h, r, w_gate, w_up, conv_w, conv_b, w_down, layer, seq_len):
    t = r.shape[0]
    tm = TOKEN_TILE
    tf = FF_TILE
    per = tm // CONV_HALO
    return pl.pallas_call(
        functools.partial(_ffn_kernel, tiles_per_seq=seq_len // tm),
        grid=(t // tm, D_FF // tf),
        in_specs=[pl.BlockSpec((tm, D_MODEL), lambda i, f: (i, 0)),
                  pl.BlockSpec((CONV_HALO, D_MODEL), lambda i, f: (jnp.maximum(i * per - 1, 0), 0)),
                  pl.BlockSpec((tm, D_MODEL), lambda i, f: (i, 0)),
                  pl.BlockSpec((None, D_MODEL, tf), lambda i, f: (layer, 0, f)),
                  pl.BlockSpec((None, D_MODEL, tf), lambda i, f: (layer, 0, f)),
                  pl.BlockSpec((None, 3, tf), lambda i, f: (layer, 0, f)),
                  pl.BlockSpec((None, 1, tf), lambda i, f: (layer, 0, f)),
                  pl.BlockSpec((None, tf, D_MODEL), lambda i, f: (layer, f, 0))],
        out_specs=pl.BlockSpec((tm, D_MODEL), lambda i, f: (i, 0)),
        out_shape=jax.ShapeDtypeStruct((t, D_MODEL), F32),
        scratch_shapes=[pltpu.VMEM((tm + CONV_HALO, D_MODEL), BF16),
                        pltpu.VMEM((tm + CONV_HALO, tf), F32)],
        compiler_params=_params("parallel", "arbitrary"),
        name="conv_mlp",
    )(h, h, r, w_gate, w_up, conv_w, conv_b, w_down)


def _ple_kernel(r_ref, p_ref, g_ref, wg_ref, wp_ref, o_ref):
    r = r_ref[...]
    gate = _sigmoid(_dot(_rms(r, g_ref[...]).astype(BF16), wg_ref[...]))
    emb = _dot(p_ref[...].astype(BF16), wp_ref[...])
    o_ref[...] = r + emb * gate


def _ple(r, p, gain, w_gate, w_proj, layer):
    t = r.shape[0]
    tm = TOKEN_TILE
    row = lambda i: (i, 0)
    return pl.pallas_call(
        _ple_kernel,
        grid=(t // tm,),
        in_specs=[pl.BlockSpec((tm, D_MODEL), row),
                  pl.BlockSpec((None, tm, PLE_DIM), lambda i: (layer, i, 0)),
                  pl.BlockSpec((None, 1, D_MODEL), lambda i: (layer, 0, 0)),
                  pl.BlockSpec((None, D_MODEL, D_MODEL), lambda i: (layer, 0, 0)),
                  pl.BlockSpec((None, PLE_DIM, D_MODEL), lambda i: (layer, 0, 0))],
        out_specs=pl.BlockSpec((tm, D_MODEL), row),
        out_shape=jax.ShapeDtypeStruct((t, D_MODEL), F32),
        compiler_params=_params("parallel"),
        name="ple",
    )(r, p, gain, w_gate, w_proj)


def _final_norm_kernel(r_ref, g_ref, o_ref):
    o_ref[...] = _rms(r_ref[...], g_ref[...])


def _final_norm(r, gain):
    t = r.shape[0]
    tm = TOKEN_TILE
    return pl.pallas_call(
        _final_norm_kernel,
        grid=(t // tm,),
        in_specs=[pl.BlockSpec((tm, D_MODEL), lambda i: (i, 0)), pl.BlockSpec((1, D_MODEL), lambda i: (0, 0))],
        out_specs=pl.BlockSpec((tm, D_MODEL), lambda i: (i, 0)),
        out_shape=jax.ShapeDtypeStruct((t, D_MODEL), F32),
        compiler_params=_params("parallel"),
        name="final_norm",
    )(r, gain)


def kernel(x, p, positions, attn_norm, w_in, hgrn_lb, hgrn_gnorm, attn_sinks, w_out, ffn_norm, w_gate, w_up,
           conv_w, conv_b, w_down, ple_norm, w_ple_gate, w_ple_proj, final_norm):
    b, s, d = x.shape
    depth = w_in.shape[0]
    t = b * s
    assert d == D_MODEL and s % MIX_TILE == 0 and s % TOKEN_TILE == 0 and MIX_TILE % RET_CHUNK == 0

    vec = lambda a: a.reshape(depth, 1, a.shape[-1])
    bf = lambda a: a.astype(BF16)

    lb_all = pl.pallas_call(
        _lower_bound_kernel, out_shape=jax.ShapeDtypeStruct(hgrn_lb.shape, F32), name="hgrn_lower_bound",
    )(hgrn_lb.astype(F32))

    inv = 1.0 / (ROPE_THETA ** (jnp.arange(0, HEAD_DIM, 2, dtype=F32) / HEAD_DIM))
    inv = jnp.tile(inv, 4).reshape(1, 128)
    tt = TOKEN_TILE
    cos, sin_a, sin_b = pl.pallas_call(
        _rope_table_kernel,
        grid=(t // tt,),
        in_specs=[pl.BlockSpec((tt, 1), lambda i: (i, 0)), pl.BlockSpec((1, 128), lambda i: (0, 0))],
        out_specs=[pl.BlockSpec((tt, 128), lambda i: (i, 0))] * 3,
        out_shape=[jax.ShapeDtypeStruct((t, 128), F32)] * 3,
        compiler_params=_params("parallel"),
        name="rope_tables",
    )(positions.reshape(t, 1), inv)

    nhead_c = C_WIDTH // HEAD_DIM
    log_g = jnp.log(1.0 - 2.0 ** (-5.0 - jnp.arange(nhead_c, dtype=F32)))

    w_in_b, w_out_b, w_gate_b, w_up_b, w_down_b = bf(w_in), bf(w_out), bf(w_gate), bf(w_up), bf(w_down)
    w_pg_b, w_pp_b = bf(w_ple_gate), bf(w_ple_proj)
    attn_norm_v, ffn_norm_v, ple_norm_v = vec(attn_norm), vec(ffn_norm), vec(ple_norm)
    gnorm_v, conv_b_v, lb_v = vec(hgrn_gnorm), vec(conv_b), vec(lb_all)
    p_flat = p.reshape(depth, t, PLE_DIM)

    r = x.reshape(t, d).astype(F32)
    for i in range(depth):
        zt, lf = _inproj(r, attn_norm_v, w_in_b, cos, sin_a, sin_b, lb_v, i)
        zt3 = zt.reshape(b, s, D_IN)
        ya = _hgrn(zt3, lf.reshape(b, s, A_WIDTH), gnorm_v, i)
        yb = _swa(zt3, attn_sinks[i].astype(F32))
        yc = _retention(zt3, log_g)
        r, h2 = _outproj(r, ya.reshape(t, A_WIDTH), yb.reshape(t, B_WIDTH), yc.reshape(t, C_WIDTH),
                         w_out_b, ffn_norm_v, i)
        r = _ffn(h2, r, w_gate_b, w_up_b, conv_w, conv_b_v, w_down_b, i, s)
        r = _ple(r, p_flat, ple_norm_v, w_pg_b, w_pp_b, i)
    out = _final_norm(r, final_norm.reshape(1, d))
    return out.reshape(b, s, d).astype(x.dtype)
```

```python
import functools

import jax
import jax.numpy as jnp
from jax import lax
from jax.experimental import pallas as pl
from jax.experimental.pallas import tpu as pltpu

F32 = jnp.float32
BF16 = jnp.bfloat16

D_MODEL = 1024
HEAD_DIM = 64
A_WIDTH = 256
B_WIDTH = 512
B_KV_WIDTH = 128
C_WIDTH = 256
WINDOW = 128
D_IN = 2816
D_FF = 2816
PLE_DIM = 256
ROPE_THETA = 10000.0
EPS = 1e-6
NEG_BIG = -1e30
LOG2E = 1.4426950408889634

AQ, AF, AI, AG = 0, 256, 512, 768
BQ, BK, BV = 1024, 1536, 1664
CQ, CK, CV, CG = 1792, 2048, 2304, 2560

TOKEN_TILE = 512
INPROJ_TILE = 1024
MIX_TILE = 512
HGRN_TILE = 1024
HGRN_CHUNK = 64
RET_CHUNK = 128
FF_SUB = 256
CONV_HALO = 16
VMEM_LIMIT = 56 * 1024 * 1024


def _dot(a, b):
    return jnp.dot(a, b, preferred_element_type=F32)


def _dot_nt(a, b):
    return lax.dot_general(a, b, (((1,), (1,)), ((), ())), preferred_element_type=F32)


def _dot_tn(a, b):
    return lax.dot_general(a, b, (((0,), (0,)), ((), ())), preferred_element_type=F32)


def _iota(shape, dim):
    return lax.broadcasted_iota(jnp.int32, shape, dim)


def _sigmoid(x):
    return 1.0 / (1.0 + jnp.exp(-x))


def _rms(x, gain):
    y = x * lax.rsqrt(jnp.mean(x * x, axis=-1, keepdims=True) + EPS)
    return y * gain


def _params(*sem):
    return pltpu.CompilerParams(dimension_semantics=sem, vmem_limit_bytes=VMEM_LIMIT)


def _lower_bound_kernel(lb_ref, o_ref):
    x = lb_ref[...]
    depth = x.shape[0]
    e = jnp.exp(x - jnp.max(x, axis=0, keepdims=True))
    p = e / jnp.sum(e, axis=0, keepdims=True)
    row = _iota(x.shape, 0)
    acc = jnp.zeros_like(x)
    for j in range(depth):
        acc = acc + jnp.where(row >= j, p[j:j + 1, :], 0.0)
    o_ref[...] = acc - p[0:1, :]


def _rope_table_kernel(pos_ref, inv_ref, cos_ref, sa_ref, sb_ref):
    ang = pos_ref[...].astype(F32) * inv_ref[...]
    c = jnp.cos(ang)
    s = jnp.sin(ang)
    first_half = (_iota(ang.shape, 1) & (HEAD_DIM - 1)) < HEAD_DIM // 2
    cos_ref[...] = c
    sa_ref[...] = jnp.where(first_half, -s, 0.0)
    sb_ref[...] = jnp.where(first_half, 0.0, s)


def _inproj_kernel(x_ref, g_ref, w_ref, cos_ref, sa_ref, sb_ref, lb_ref, zt_ref, lf_ref):
    h = _rms(x_ref[...], g_ref[...]).astype(BF16)
    cos = cos_ref[...]
    sa = sa_ref[...]
    sb = sb_ref[...]

    def proj(c0, width):
        return _dot(h, w_ref[:, c0:c0 + width])

    def rope(z):
        return z * cos + pltpu.roll(z, 96, axis=1) * sa + pltpu.roll(z, 32, axis=1) * sb

    def put(c0, val):
        zt_ref[:, c0:c0 + val.shape[1]] = val.astype(BF16)

    def put_rope(c0, z, scale):
        for j in range(z.shape[1] // 128):
            put(c0 + 128 * j, rope(z[:, 128 * j:128 * (j + 1)]) * scale)

    put(AQ, proj(AQ, A_WIDTH))
    fl = proj(AF, A_WIDTH)
    e = jnp.exp(-jnp.abs(fl))
    inv = 1.0 / (1.0 + e)
    sig_pos = jnp.where(fl >= 0, inv, e * inv)
    sig_neg = jnp.where(fl >= 0, e * inv, inv)
    lb = lb_ref[...]
    lf_ref[...] = jnp.log(lb + (1.0 - lb) * sig_pos)
    put(AF, (1.0 - lb) * sig_neg)
    put(AI, proj(AI, A_WIDTH))
    g = proj(AG, A_WIDTH)
    put(AG, g * _sigmoid(g))
    put_rope(BQ, proj(BQ, B_WIDTH), HEAD_DIM ** -0.5 * LOG2E)
    kv = proj(BK, 2 * B_KV_WIDTH)
    put_rope(BK, kv[:, :B_KV_WIDTH], 1.0)
    put(BV, kv[:, B_KV_WIDTH:])
    put_rope(CQ, proj(CQ, C_WIDTH), 1.0)
    put_rope(CK, proj(CK, C_WIDTH), HEAD_DIM ** -0.5)
    put(CV, proj(CV, C_WIDTH))
    g = proj(CG, C_WIDTH)
    put(CG, g * _sigmoid(g))


def _inproj(r, gain, w_in, cos, sa, sb, lb_all, layer):
    t = r.shape[0]
    tm = INPROJ_TILE
    row = lambda i: (i, 0)
    return pl.pallas_call(
        _inproj_kernel,
        grid=(t // tm,),
        in_specs=[
            pl.BlockSpec((tm, D_MODEL), row),
            pl.BlockSpec((None, 1, D_MODEL), lambda i: (layer, 0, 0)),
            pl.BlockSpec((None, D_MODEL, D_IN), lambda i: (layer, 0, 0)),
            pl.BlockSpec((tm, 128), row),
            pl.BlockSpec((tm, 128), row),
            pl.BlockSpec((tm, 128), row),
            pl.BlockSpec((None, 1, A_WIDTH), lambda i: (layer, 0, 0)),
        ],
        out_specs=[pl.BlockSpec((tm, D_IN), row), pl.BlockSpec((tm, A_WIDTH), row)],
        out_shape=[jax.ShapeDtypeStruct((t, D_IN), BF16), jax.ShapeDtypeStruct((t, A_WIDTH), F32)],
        compiler_params=_params("parallel"),
        name="inproj",
    )(r, gain, w_in, cos, sa, sb, lb_all)


def _hgrn_kernel(q_ref, k_ref, v_ref, sg_ref, lf_ref, gn_ref, o_ref, st_ref, m3_ref, lev_ref, *, nchunk):
    width = A_WIDTH
    cl = HGRN_CHUNK
    nlev = cl.bit_length() - 1
    nhead = width // HEAD_DIM

    @pl.when(pl.program_id(1) == 0)
    def _():
        st_ref[...] = jnp.zeros_like(st_ref)
        t = _iota((cl, 3 * cl), 0)
        s = _iota((cl, 3 * cl), 1) & (cl - 1)
        m3_ref[0:cl, :] = (s <= t).astype(BF16)
        m3_ref[cl:2 * cl, :] = (s <= (t & -4) + 1).astype(BF16)
        tq = _iota((cl, width), 0)
        sk = _iota((cl, width), 1) & (cl - 1)
        diff = tq ^ sk
        lvl = jnp.zeros((cl, width), jnp.int32)
        for j in range(1, nlev):
            lvl = lvl + (diff >= (1 << j)).astype(jnp.int32)
        lev_ref[...] = jnp.where(tq > sk, lvl, jnp.where(tq == sk, -2, -1))

    row = _iota((cl, width), 0)
    head = _iota((cl, width), 1) >> 6
    hmask = [(head == h).astype(BF16) for h in range(nhead)]
    same_head = (_iota((width, width), 0) >> 6) == (_iota((width, width), 1) >> 6)
    sel = same_head.astype(BF16)
    gain = gn_ref[...]

    def by_head(x):
        return jnp.concatenate([x * hmask[h] for h in range(nhead)], axis=0)

    def sums_stage(c):
        sl = slice(c * cl, (c + 1) * cl)
        lf = lf_ref[sl, :]
        hi = lf.astype(BF16)
        rem = lf - hi.astype(F32)
        mid = rem.astype(BF16)
        lo = (rem - mid.astype(F32)).astype(BF16)
        return dict(sl=sl, lf=lf, sums=_dot(m3_ref[...], jnp.concatenate([hi, mid, lo], axis=0)))

    def level_stage(ch):
        sl, lf, sums = ch["sl"], ch["lf"], ch["sums"]
        qf = q_ref[sl, :].astype(F32)
        kf = k_ref[sl, :].astype(F32)
        bc = sums[0:cl]
        bl = bc[cl - 1:cl, :]
        scores = []
        for l in range(nlev):
            m = 1 << l
            second = (row & m) != 0
            if l == 0:
                e = jnp.where(second, jnp.exp(lf), 1.0)
            else:
                if l == 1:
                    ref = sums[cl:2 * cl]
                else:
                    ref = jnp.concatenate(
                        [jnp.broadcast_to(bc[g + m - 1:g + m, :], (2 * m, width)) for g in range(0, cl, 2 * m)], axis=0)
                e = jnp.exp(-jnp.abs(bc - ref))
            x = (jnp.where(second, qf, kf) * e).astype(BF16)
            scores.append(_dot_nt(x, by_head(x)))
        ch.update(scores=scores, diag=_dot((qf * kf).astype(BF16), sel),
                  qe=(qf * jnp.exp(bc)).astype(BF16), kd=(kf * jnp.exp(bl - bc)).astype(BF16), ebl=jnp.exp(bl))

    def state_stage(ch):
        vb = v_ref[ch["sl"], :]
        st = st_ref[...]
        ch["o"] = _dot_nt(ch["qe"], st.astype(BF16))
        st_ref[...] = st * ch["ebl"] + jnp.where(same_head, _dot_tn(vb, ch["kd"]), 0.0)

    def value_stage(ch):
        lev = lev_ref[...]
        sc = jnp.zeros((cl, width), F32)
        for l in range(nlev):
            sc = jnp.where(lev == l, ch["scores"][l], sc)
        sc = jnp.where(lev == -2, ch["diag"], sc)
        ch["o"] = ch["o"] + _dot(sc.astype(BF16), by_head(v_ref[ch["sl"], :]))

    def norm_stage(ch):
        o = ch["o"]
        ms = _dot((o * o).astype(BF16), sel) * (1.0 / HEAD_DIM)
        y = o * lax.rsqrt(ms + EPS) * gain * sg_ref[ch["sl"], :].astype(F32)
        o_ref[ch["sl"], :] = y.astype(BF16)

    chunks = [None] * nchunk
    chunks[0] = sums_stage(0)
    for c in range(nchunk):
        if c + 1 < nchunk:
            chunks[c + 1] = sums_stage(c + 1)
        level_stage(chunks[c])
        state_stage(chunks[c])
        if c > 0:
            norm_stage(chunks[c - 1])
            chunks[c - 1] = None
        value_stage(chunks[c])
    norm_stage(chunks[nchunk - 1])


def _hgrn(zt, lf, gnorm, layer):
    b, s, _ = zt.shape
    tb = HGRN_TILE
    col = lambda j: pl.BlockSpec((None, tb, A_WIDTH), lambda bi, si: (bi, si, j))
    return pl.pallas_call(
        functools.partial(_hgrn_kernel, nchunk=tb // HGRN_CHUNK),
        grid=(b, s // tb),
        in_specs=[col(AQ // A_WIDTH), col(AF // A_WIDTH), col(AI // A_WIDTH), col(AG // A_WIDTH),
                  pl.BlockSpec((None, tb, A_WIDTH), lambda bi, si: (bi, si, 0)),
                  pl.BlockSpec((None, 1, A_WIDTH), lambda bi, si: (layer, 0, 0))],
        out_specs=pl.BlockSpec((None, tb, A_WIDTH), lambda bi, si: (bi, si, 0)),
        out_shape=jax.ShapeDtypeStruct((b, s, A_WIDTH), BF16),
        scratch_shapes=[pltpu.VMEM((A_WIDTH, A_WIDTH), F32),
                        pltpu.VMEM((2 * HGRN_CHUNK, 3 * HGRN_CHUNK), BF16),
                        pltpu.VMEM((HGRN_CHUNK, A_WIDTH), jnp.int32)],
        compiler_params=_params("parallel", "arbitrary"),
        name="hgrn2",
    )(zt, zt, zt, zt, lf, gnorm)


def _ret_kernel(lg_ref, q_ref, k_ref, v_ref, sg_ref, o_ref, s_ref, d_ref, qd_ref, kd_ref, cd_ref, *, nchunk):
    width = C_WIDTH
    cl = RET_CHUNK
    nhead = width // HEAD_DIM
    head_row = _iota((1, width), 1) >> 6

    @pl.when(pl.program_id(1) == 0)
    def _():
        s_ref[...] = jnp.zeros_like(s_ref)
        lgl = jnp.zeros((1, width), F32)
        rel = _iota((cl, cl), 0) - _iota((cl, cl), 1)
        causal = rel >= 0
        relf = jnp.where(causal, rel, 0).astype(F32)
        for h in range(nhead):
            lgl = jnp.where(head_row == h, lg_ref[h], lgl)
            d_ref[:, cl * h:cl * (h + 1)] = jnp.where(causal, jnp.exp(relf * lg_ref[h]), 0.0)
        t = _iota((cl, width), 0).astype(F32)
        qd_ref[...] = jnp.exp((t + 1.0) * lgl)
        kd_ref[...] = jnp.exp((cl - 1.0 - t) * lgl)
        cd_ref[...] = jnp.exp(float(cl) * lgl)

    head = _iota((cl, width), 1) >> 6
    hmask = [(head == h).astype(BF16) for h in range(nhead)]
    same_head = (_iota((width, width), 0) >> 6) == (_iota((width, width), 1) >> 6)
    sel = same_head.astype(BF16)

    def by_head(x):
        return jnp.concatenate([x * hmask[h] for h in range(nhead)], axis=0)

    def score_stage(c):
        sl = slice(c * cl, (c + 1) * cl)
        return dict(sl=sl, scores=_dot_nt(q_ref[sl, :], by_head(k_ref[sl, :])))

    def state_stage(ch):
        sl = ch["sl"]
        st = s_ref[...]
        ch["o"] = _dot((q_ref[sl, :].astype(F32) * qd_ref[...]).astype(BF16), st.astype(BF16))
        kdec = (k_ref[sl, :].astype(F32) * kd_ref[...]).astype(BF16)
        s_ref[...] = st * cd_ref[...] + jnp.where(same_head, _dot_tn(kdec, v_ref[sl, :]), 0.0)

    def value_stage(ch):
        sc = ch["scores"] * d_ref[...]
        ch["o"] = ch["o"] + _dot(sc.astype(BF16), by_head(v_ref[ch["sl"], :]))

    def norm_stage(ch):
        o = ch["o"]
        ms = _dot((o * o).astype(BF16), sel) * (1.0 / HEAD_DIM)
        y = o * lax.rsqrt(ms + EPS) * sg_ref[ch["sl"], :].astype(F32)
        o_ref[ch["sl"], :] = y.astype(BF16)

    chunks = [None] * nchunk
    chunks[0] = score_stage(0)
    for c in range(nchunk):
        if c + 1 < nchunk:
            chunks[c + 1] = score_stage(c + 1)
        state_stage(chunks[c])
        if c > 0:
            norm_stage(chunks[c - 1])
            chunks[c - 1] = None
        value_stage(chunks[c])
    norm_stage(chunks[nchunk - 1])


def _retention(zt, log_g):
    b, s, _ = zt.shape
    tb = MIX_TILE
    cl = RET_CHUNK
    col = lambda j: pl.BlockSpec((None, tb, C_WIDTH), lambda bi, si: (bi, si, j))
    return pl.pallas_call(
        functools.partial(_ret_kernel, nchunk=tb // cl),
        grid=(b, s // tb),
        in_specs=[pl.BlockSpec(memory_space=pltpu.SMEM),
                  col(CQ // C_WIDTH), col(CK // C_WIDTH), col(CV // C_WIDTH), col(CG // C_WIDTH)],
        out_specs=pl.BlockSpec((None, tb, C_WIDTH), lambda bi, si: (bi, si, 0)),
        out_shape=jax.ShapeDtypeStruct((b, s, C_WIDTH), BF16),
        scratch_shapes=[pltpu.VMEM((C_WIDTH, C_WIDTH), F32),
                        pltpu.VMEM((cl, C_WIDTH // HEAD_DIM * cl), F32),
                        pltpu.VMEM((cl, C_WIDTH), F32),
                        pltpu.VMEM((cl, C_WIDTH), F32),
                        pltpu.VMEM((1, C_WIDTH), F32)],
        compiler_params=_params("parallel", "arbitrary"),
        name="retention",
    )(log_g, zt, zt, zt, zt)


def _swa_kernel(sink_ref, q_ref, k_ref, v_ref, kp_ref, vp_ref, o_ref, *, nblk):
    w = WINDOW
    first_tile = pl.program_id(1) == 0
    low = _iota((2 * w, 128), 1) < HEAD_DIM
    qi = _iota((w, 2 * w), 0)
    kj = _iota((w, 2 * w), 1)
    rel = qi + w - kj
    band = (rel >= 0) & (rel < w)
    band_first = band & ((kj >= w) | jnp.logical_not(first_tile))
    ones = jnp.ones((2 * w, 128), BF16)
    zero = jnp.zeros((2 * w, 128), BF16)

    nhead = B_WIDTH // HEAD_DIM

    def operands(blk):
        cur = slice(blk * w, (blk + 1) * w)
        if blk == 0:
            kprev, vprev, valid = kp_ref[...], vp_ref[...], band_first
        else:
            prev = slice((blk - 1) * w, blk * w)
            kprev, vprev, valid = k_ref[prev, :], v_ref[prev, :], band
        kk = jnp.concatenate([kprev, k_ref[cur, :]], axis=0)
        vv = jnp.concatenate([vprev, v_ref[cur, :]], axis=0)
        kr = pltpu.roll(kk.astype(F32), HEAD_DIM, axis=1).astype(BF16)
        vr = pltpu.roll(vv.astype(F32), HEAD_DIM, axis=1).astype(BF16)
        kop = {}
        vop = {}
        for g in range(2):
            for half in range(2):
                keep = low if half == 0 else jnp.logical_not(low)
                kop[g, half] = jnp.where(keep, kk if g == half else kr, zero)
                vop[g, half] = jnp.concatenate([jnp.where(keep, vv if g == half else vr, zero), ones], axis=1)
        return dict(cur=cur, valid=valid, kop=kop, vop=vop)

    def scores(blk_ops, h):
        qcol = q_ref[blk_ops["cur"], 128 * (h // 2):128 * (h // 2 + 1)]
        return _dot_nt(qcol, blk_ops["kop"][h // 4, h % 2])

    items = [(blk, h) for blk in range(nblk) for h in range(nhead)]
    ops = operands(0)
    nxt = scores(ops, 0)
    acc = None
    for n, (blk, h) in enumerate(items):
        cur_ops, raw = ops, nxt
        if n + 1 < len(items):
            if items[n + 1][0] != blk:
                ops = operands(items[n + 1][0])
            nxt = scores(ops, items[n + 1][1])
        sink = sink_ref[h] * LOG2E
        s = jnp.where(cur_ops["valid"], raw, NEG_BIG)
        m = jnp.maximum(jnp.max(s, axis=-1, keepdims=True), sink)
        p = jnp.exp2(s - m).astype(BF16)
        ol = _dot(p, cur_ops["vop"][h // 4, h % 2])
        part = ol[:, :128] / (ol[:, 128:] + jnp.exp2(sink - m))
        if h % 2 == 0:
            acc = part
        else:
            o_ref[cur_ops["cur"], 128 * (h // 2):128 * (h // 2 + 1)] = (acc + part).astype(BF16)


def _swa(zt, sinks):
    b, s, _ = zt.shape
    tb = MIX_TILE
    per = tb // WINDOW
    prev = lambda j: pl.BlockSpec((None, WINDOW, B_KV_WIDTH),
                                  lambda bi, si: (bi, jnp.maximum(si * per - 1, 0), j))
    return pl.pallas_call(
        functools.partial(_swa_kernel, nblk=per),
        grid=(b, s // tb),
        in_specs=[pl.BlockSpec(memory_space=pltpu.SMEM),
                  pl.BlockSpec((None, tb, B_WIDTH), lambda bi, si: (bi, si, BQ // B_WIDTH)),
                  pl.BlockSpec((None, tb, B_KV_WIDTH), lambda bi, si: (bi, si, BK // B_KV_WIDTH)),
                  pl.BlockSpec((None, tb, B_KV_WIDTH), lambda bi, si: (bi, si, BV // B_KV_WIDTH)),
                  prev(BK // B_KV_WIDTH), prev(BV // B_KV_WIDTH)],
        out_specs=pl.BlockSpec((None, tb, B_WIDTH), lambda bi, si: (bi, si, 0)),
        out_shape=jax.ShapeDtypeStruct((b, s, B_WIDTH), BF16),
        compiler_params=_params("parallel", "arbitrary"),
        name="swa",
    )(sinks, zt, zt, zt, zt, zt)


def _tail_kernel(r_ref, ya_ref, yb_ref, yc_ref, p_ref, wo_ref, gf_ref, wg_ref, wu_ref, cw_ref, cb_ref, wd_ref,
                 gp_ref, wpg_ref, wpp_ref, gfin_ref, o_ref, lhs_ref, ge_ref, act_ref, halo_ref,
                 *, tiles_per_seq, final):
    tm = r_ref.shape[0]
    b0 = A_WIDTH
    c0 = A_WIDTH + B_WIDTH
    r1 = r_ref[...] + _dot(ya_ref[...], wo_ref[0:b0, :])
    r1 = r1 + _dot(yb_ref[...], wo_ref[b0:c0, :])
    r1 = r1 + _dot(yc_ref[...], wo_ref[c0:, :])
    h2 = _rms(r1, gf_ref[...]).astype(BF16)

    seq_start = pl.program_id(0) % tiles_per_seq == 0

    @pl.when(seq_start)
    def _():
        lhs_ref[0:CONV_HALO, :] = jnp.zeros((CONV_HALO, D_MODEL), BF16)

    @pl.when(jnp.logical_not(seq_start))
    def _():
        lhs_ref[0:CONV_HALO, :] = halo_ref[...]

    lhs_ref[CONV_HALO:, :] = h2
    halo_ref[...] = h2[tm - CONV_HALO:, :]

    for j in range(D_FF // FF_SUB):
        cols = slice(j * FF_SUB, (j + 1) * FF_SUB)
        ge = ge_ref.at[j % 2]
        ge[...] = _dot(lhs_ref[...], wg_ref[:, cols])
        cw = cw_ref[:, cols]
        gate = (ge[pl.ds(CONV_HALO - 2, tm), :] * cw[0:1, :]
                + ge[pl.ds(CONV_HALO - 1, tm), :] * cw[1:2, :]
                + ge[pl.ds(CONV_HALO, tm), :] * cw[2:3, :]
                + cb_ref[:, cols])
        up = _dot(lhs_ref[CONV_HALO:, :], wu_ref[:, cols])
        act_ref[:, cols] = (jax.nn.gelu(gate, approximate=True) * up).astype(BF16)
    r2 = r1 + _dot(act_ref[...], wd_ref[...])

    gate = _sigmoid(_dot(_rms(r2, gp_ref[...]).astype(BF16), wpg_ref[...]))
    r3 = r2 + _dot(p_ref[...].astype(BF16), wpp_ref[...]) * gate
    o_ref[...] = _rms(r3, gfin_ref[...]) if final else r3


def _tail(r, ya, yb, yc, p, w_out, ffn_gain, w_gate, w_up, conv_w, conv_b, w_down, ple_gain, w_pg, w_pp,
          final_gain, layer, seq_len, final):
    t = r.shape[0]
    tm = TOKEN_TILE
    row = lambda i: (i, 0)
    once = pl.Buffered(1)

    def whole(shape):
        return pl.BlockSpec((None,) + shape, lambda i: (layer,) + (0,) * len(shape), pipeline_mode=once)

    return pl.pallas_call(
        functools.partial(_tail_kernel, tiles_per_seq=seq_len // tm, final=final),
        grid=(t // tm,),
        in_specs=[pl.BlockSpec((tm, D_MODEL), row),
                  pl.BlockSpec((tm, A_WIDTH), row),
                  pl.BlockSpec((tm, B_WIDTH), row),
                  pl.BlockSpec((tm, C_WIDTH), row),
                  pl.BlockSpec((None, tm, PLE_DIM), lambda i: (layer, i, 0)),
                  whole((D_MODEL, D_MODEL)),
                  whole((1, D_MODEL)),
                  whole((D_MODEL, D_FF)),
                  whole((D_MODEL, D_FF)),
                  whole((3, D_FF)),
                  whole((1, D_FF)),
                  whole((D_FF, D_MODEL)),
                  whole((1, D_MODEL)),
                  whole((D_MODEL, D_MODEL)),
                  whole((PLE_DIM, D_MODEL)),
                  pl.BlockSpec((1, D_MODEL), lambda i: (0, 0), pipeline_mode=once)],
        out_specs=pl.BlockSpec((tm, D_MODEL), row),
        out_shape=jax.ShapeDtypeStruct((t, D_MODEL), F32),
        scratch_shapes=[pltpu.VMEM((tm + CONV_HALO, D_MODEL), BF16),
                        pltpu.VMEM((2, tm + CONV_HALO, FF_SUB), F32),
                        pltpu.VMEM((tm, D_FF), BF16),
                        pltpu.VMEM((CONV_HALO, D_MODEL), BF16)],
        compiler_params=_params("arbitrary"),
        name="layer_tail",
    )(r, ya, yb, yc, p, w_out, ffn_gain, w_gate, w_up, conv_w, conv_b, w_down, ple_gain, w_pg, w_pp, final_gain)


def kernel(x, p, positions, attn_norm, w_in, hgrn_lb, hgrn_gnorm, attn_sinks, w_out, ffn_norm, w_gate, w_up,
           conv_w, conv_b, w_down, ple_norm, w_ple_gate, w_ple_proj, final_norm):
    b, s, d = x.shape
    depth = w_in.shape[0]
    t = b * s
    assert d == D_MODEL and s % MIX_TILE == 0 and s % HGRN_TILE == 0 and s % TOKEN_TILE == 0
    assert t % INPROJ_TILE == 0

    vec = lambda a: a.reshape(depth, 1, a.shape[-1])
    bf = lambda a: a.astype(BF16)

    lb_all = pl.pallas_call(
        _lower_bound_kernel, out_shape=jax.ShapeDtypeStruct(hgrn_lb.shape, F32), name="hgrn_lower_bound",
    )(hgrn_lb.astype(F32))

    inv = 1.0 / (ROPE_THETA ** (jnp.arange(0, HEAD_DIM, 2, dtype=F32) / HEAD_DIM))
    inv = jnp.tile(inv, 4).reshape(1, 128)
    tt = INPROJ_TILE
    cos, sin_a, sin_b = pl.pallas_call(
        _rope_table_kernel,
        grid=(t // tt,),
        in_specs=[pl.BlockSpec((tt, 1), lambda i: (i, 0)), pl.BlockSpec((1, 128), lambda i: (0, 0))],
        out_specs=[pl.BlockSpec((tt, 128), lambda i: (i, 0))] * 3,
        out_shape=[jax.ShapeDtypeStruct((t, 128), F32)] * 3,
        compiler_params=_params("parallel"),
        name="rope_tables",
    )(positions.reshape(t, 1), inv)

    nhead_c = C_WIDTH // HEAD_DIM
    log_g = jnp.log(1.0 - 2.0 ** (-5.0 - jnp.arange(nhead_c, dtype=F32)))

    w_in_b, w_out_b, w_gate_b, w_up_b, w_down_b = bf(w_in), bf(w_out), bf(w_gate), bf(w_up), bf(w_down)
    w_pg_b, w_pp_b = bf(w_ple_gate), bf(w_ple_proj)
    attn_norm_v, ffn_norm_v, ple_norm_v = vec(attn_norm), vec(ffn_norm), vec(ple_norm)
    gnorm_v, conv_b_v, lb_v = vec(hgrn_gnorm), vec(conv_b), vec(lb_all)
    p_flat = p.reshape(depth, t, PLE_DIM)

    r = x.reshape(t, d).astype(F32)
    for i in range(depth):
        zt, lf = _inproj(r, attn_norm_v, w_in_b, cos, sin_a, sin_b, lb_v, i)
        zt3 = zt.reshape(b, s, D_IN)
        ya = _hgrn(zt3, lf.reshape(b, s, A_WIDTH), gnorm_v, i)
        yb = _swa(zt3, attn_sinks[i].astype(F32))
        yc = _retention(zt3, log_g)
        r = _tail(r, ya.reshape(t, A_WIDTH), yb.reshape(t, B_WIDTH), yc.reshape(t, C_WIDTH), p_flat,
                  w_out_b, ffn_norm_v, w_gate_b, w_up_b, conv_w, conv_b_v, w_down_b, ple_norm_v, w_pg_b, w_pp_b,
                  final_norm.reshape(1, d), i, s, i == depth - 1)
    return r.reshape(b, s, d).astype(x.dtype)
```

```python
import functools

import jax
import jax.numpy as jnp
from jax import lax
from jax.experimental import pallas as pl
from jax.experimental.pallas import tpu as pltpu

F32 = jnp.float32
BF16 = jnp.bfloat16

D_MODEL = 1024
HEAD_DIM = 64
A_WIDTH = 256
B_WIDTH = 512
B_KV_WIDTH = 128
C_WIDTH = 256
WINDOW = 128
D_IN = 2816
D_FF = 2816
PLE_DIM = 256
ROPE_THETA = 10000.0
EPS = 1e-6
NEG_BIG = -1e30
LOG2E = 1.4426950408889634

AQ, AF, AI, AG = 0, 256, 512, 768
BQ, BK, BV = 1024, 1536, 1664
CQ, CK, CV, CG = 1792, 2048, 2304, 2560

TOKEN_TILE = 512
INPROJ_TILE = 1024
SWA_LOOKAHEAD = 2
HGRN_CHUNK = 64
RET_CHUNK = 128
FF_SUB = 256
CONV_HALO = 16
VMEM_LIMIT = 56 * 1024 * 1024


def _dot(a, b):
    return jnp.dot(a, b, preferred_element_type=F32)


def _dot_nt(a, b):
    return lax.dot_general(a, b, (((1,), (1,)), ((), ())), preferred_element_type=F32)


def _dot_tn(a, b):
    return lax.dot_general(a, b, (((0,), (0,)), ((), ())), preferred_element_type=F32)


def _iota(shape, dim):
    return lax.broadcasted_iota(jnp.int32, shape, dim)


def _sigmoid(x):
    return 1.0 / (1.0 + jnp.exp(-x))


def _rms(x, gain):
    y = x * lax.rsqrt(jnp.mean(x * x, axis=-1, keepdims=True) + EPS)
    return y * gain


def _params(*sem):
    return pltpu.CompilerParams(dimension_semantics=sem, vmem_limit_bytes=VMEM_LIMIT)


def _lower_bound_kernel(lb_ref, o_ref):
    x = lb_ref[...]
    depth = x.shape[0]
    e = jnp.exp(x - jnp.max(x, axis=0, keepdims=True))
    p = e / jnp.sum(e, axis=0, keepdims=True)
    row = _iota(x.shape, 0)
    acc = jnp.zeros_like(x)
    for j in range(depth):
        acc = acc + jnp.where(row >= j, p[j:j + 1, :], 0.0)
    o_ref[...] = acc - p[0:1, :]


def _rope_table_kernel(pos_ref, inv_ref, cos_ref, sa_ref, sb_ref):
    ang = pos_ref[...].astype(F32) * inv_ref[...]
    c = jnp.cos(ang)
    s = jnp.sin(ang)
    first_half = (_iota(ang.shape, 1) & (HEAD_DIM - 1)) < HEAD_DIM // 2
    cos_ref[...] = c
    sa_ref[...] = jnp.where(first_half, -s, 0.0)
    sb_ref[...] = jnp.where(first_half, 0.0, s)


def _inproj_kernel(x_ref, g_ref, w_ref, cos_ref, sa_ref, sb_ref, lb_ref, zt_ref, lf_ref):
    h = _rms(x_ref[...], g_ref[...]).astype(BF16)
    cos = cos_ref[...]
    sa = sa_ref[...]
    sb = sb_ref[...]

    def proj(c0, width):
        return _dot(h, w_ref[:, c0:c0 + width])

    def rope(z):
        return z * cos + pltpu.roll(z, 96, axis=1) * sa + pltpu.roll(z, 32, axis=1) * sb

    def put(c0, val):
        zt_ref[:, c0:c0 + val.shape[1]] = val.astype(BF16)

    def put_rope(c0, z, scale):
        for j in range(z.shape[1] // 128):
            put(c0 + 128 * j, rope(z[:, 128 * j:128 * (j + 1)]) * scale)

    put(AQ, proj(AQ, A_WIDTH))
    fl = proj(AF, A_WIDTH)
    e = jnp.exp(-jnp.abs(fl))
    inv = 1.0 / (1.0 + e)
    sig_pos = jnp.where(fl >= 0, inv, e * inv)
    sig_neg = jnp.where(fl >= 0, e * inv, inv)
    lb = lb_ref[...]
    lf_ref[...] = jnp.log(lb + (1.0 - lb) * sig_pos) * LOG2E
    put(AF, (1.0 - lb) * sig_neg)
    put(AI, proj(AI, A_WIDTH))
    g = proj(AG, A_WIDTH)
    put(AG, g * _sigmoid(g))
    put_rope(BQ, proj(BQ, B_WIDTH), HEAD_DIM ** -0.5 * LOG2E)
    kv = proj(BK, 2 * B_KV_WIDTH)
    put_rope(BK, kv[:, :B_KV_WIDTH], 1.0)
    put(BV, kv[:, B_KV_WIDTH:])
    put_rope(CQ, proj(CQ, C_WIDTH), 1.0)
    put_rope(CK, proj(CK, C_WIDTH), HEAD_DIM ** -0.5)
    put(CV, proj(CV, C_WIDTH))
    g = proj(CG, C_WIDTH)
    put(CG, g * _sigmoid(g))


def _inproj(r, gain, w_in, cos, sa, sb, lb_all, layer):
    t = r.shape[0]
    tm = INPROJ_TILE
    row = lambda i: (i, 0)
    return pl.pallas_call(
        _inproj_kernel,
        grid=(t // tm,),
        in_specs=[
            pl.BlockSpec((tm, D_MODEL), row),
            pl.BlockSpec((None, 1, D_MODEL), lambda i: (layer, 0, 0)),
            pl.BlockSpec((None, D_MODEL, D_IN), lambda i: (layer, 0, 0)),
            pl.BlockSpec((tm, 128), row),
            pl.BlockSpec((tm, 128), row),
            pl.BlockSpec((tm, 128), row),
            pl.BlockSpec((None, 1, A_WIDTH), lambda i: (layer, 0, 0)),
        ],
        out_specs=[pl.BlockSpec((tm, D_IN), row), pl.BlockSpec((tm, A_WIDTH), row)],
        out_shape=[jax.ShapeDtypeStruct((t, D_IN), BF16), jax.ShapeDtypeStruct((t, A_WIDTH), F32)],
        compiler_params=_params("parallel"),
        name="inproj",
    )(r, gain, w_in, cos, sa, sb, lb_all)


def _hgrn_stages(q_ref, k_ref, v_ref, sg_ref, lf_ref, gn_ref, o_ref, st_ref, m3_ref, lev_ref, nchunk):
    width = A_WIDTH
    cl = HGRN_CHUNK
    nlev = cl.bit_length() - 1
    nhead = width // HEAD_DIM

    def init():
        st_ref[...] = jnp.zeros_like(st_ref)
        t = _iota((cl, 3 * cl), 0)
        s = _iota((cl, 3 * cl), 1) & (cl - 1)
        m3_ref[0:cl, :] = (s <= t).astype(BF16)
        m3_ref[cl:2 * cl, :] = (s <= (t & -4) + 1).astype(BF16)
        tq = _iota((cl, width), 0)
        sk = _iota((cl, width), 1) & (cl - 1)
        diff = tq ^ sk
        lvl = jnp.zeros((cl, width), jnp.int32)
        for j in range(1, nlev):
            lvl = lvl + (diff >= (1 << j)).astype(jnp.int32)
        lev_ref[...] = jnp.where(tq > sk, lvl, jnp.where(tq == sk, -2, -1))

    row = _iota((cl, width), 0)
    head = _iota((cl, width), 1) >> 6
    hmask = [(head == h).astype(BF16) for h in range(nhead)]
    same_head = (_iota((width, width), 0) >> 6) == (_iota((width, width), 1) >> 6)
    sel = same_head.astype(BF16)
    gain = gn_ref[...]
    chunks = [None] * nchunk

    def by_head(x):
        return jnp.concatenate([x * hmask[h] for h in range(nhead)], axis=0)

    def sums_stage(c):
        sl = slice(c * cl, (c + 1) * cl)
        lf = lf_ref[sl, :]
        hi = lf.astype(BF16)
        rem = lf - hi.astype(F32)
        mid = rem.astype(BF16)
        lo = (rem - mid.astype(F32)).astype(BF16)
        chunks[c] = dict(sl=sl, lf=lf, sums=_dot(m3_ref[...], jnp.concatenate([hi, mid, lo], axis=0)))

    def level_stage(c):
        ch = chunks[c]
        sl, lf, sums = ch["sl"], ch["lf"], ch["sums"]
        qf = q_ref[sl, :].astype(F32)
        kf = k_ref[sl, :].astype(F32)
        bc = sums[0:cl]
        bl = bc[cl - 1:cl, :]
        scores = []
        for l in range(nlev):
            m = 1 << l
            second = (row & m) != 0
            if l == 0:
                e = jnp.where(second, jnp.exp2(lf), 1.0)
            else:
                if l == 1:
                    ref = sums[cl:2 * cl]
                else:
                    ref = jnp.concatenate(
                        [jnp.broadcast_to(bc[g + m - 1:g + m, :], (2 * m, width)) for g in range(0, cl, 2 * m)], axis=0)
                e = jnp.exp2(-jnp.abs(bc - ref))
            x = (jnp.where(second, qf, kf) * e).astype(BF16)
            scores.append(_dot_nt(x, by_head(x)))
        ch.update(scores=scores, diag=_dot((qf * kf).astype(BF16), sel),
                  qe=(qf * jnp.exp2(bc)).astype(BF16), kd=(kf * jnp.exp2(bl - bc)).astype(BF16), ebl=jnp.exp2(bl))

    def state_stage(c):
        ch = chunks[c]
        vb = v_ref[ch["sl"], :]
        st = st_ref[...]
        ch["o"] = _dot_nt(ch["qe"], st.astype(BF16))
        st_ref[...] = st * ch["ebl"] + jnp.where(same_head, _dot_tn(vb, ch["kd"]), 0.0)

    def value_stage(c):
        ch = chunks[c]
        lev = lev_ref[...]
        sc = jnp.zeros((cl, width), F32)
        for l in range(nlev):
            sc = jnp.where(lev == l, ch["scores"][l], sc)
        sc = jnp.where(lev == -2, ch["diag"], sc)
        ch["o"] = ch["o"] + _dot(sc.astype(BF16), by_head(v_ref[ch["sl"], :]))

    def norm_stage(c):
        ch = chunks[c]
        o = ch["o"]
        ms = _dot((o * o).astype(BF16), sel) * (1.0 / HEAD_DIM)
        y = o * lax.rsqrt(ms + EPS) * gain * sg_ref[ch["sl"], :].astype(F32)
        o_ref[ch["sl"], :] = y.astype(BF16)
        chunks[c] = None

    stages = [functools.partial(sums_stage, 0)]
    for c in range(nchunk):
        if c + 1 < nchunk:
            stages.append(functools.partial(sums_stage, c + 1))
        stages.append(functools.partial(level_stage, c))
        stages.append(functools.partial(state_stage, c))
        if c > 0:
            stages.append(functools.partial(norm_stage, c - 1))
        stages.append(functools.partial(value_stage, c))
    stages.append(functools.partial(norm_stage, nchunk - 1))
    return init, stages


def _ret_stages(lg_ref, q_ref, k_ref, v_ref, sg_ref, o_ref, s_ref, d_ref, qd_ref, kd_ref, cd_ref, nchunk):
    width = C_WIDTH
    cl = RET_CHUNK
    nhead = width // HEAD_DIM
    head_row = _iota((1, width), 1) >> 6

    def init():
        s_ref[...] = jnp.zeros_like(s_ref)
        lgl = jnp.zeros((1, width), F32)
        rel = _iota((cl, cl), 0) - _iota((cl, cl), 1)
        causal = rel >= 0
        relf = jnp.where(causal, rel, 0).astype(F32)
        for h in range(nhead):
            lgl = jnp.where(head_row == h, lg_ref[h], lgl)
            d_ref[:, cl * h:cl * (h + 1)] = jnp.where(causal, jnp.exp(relf * lg_ref[h]), 0.0)
        t = _iota((cl, width), 0).astype(F32)
        qd_ref[...] = jnp.exp((t + 1.0) * lgl)
        kd_ref[...] = jnp.exp((cl - 1.0 - t) * lgl)
        cd_ref[...] = jnp.exp(float(cl) * lgl)

    head = _iota((cl, width), 1) >> 6
    hmask = [(head == h).astype(BF16) for h in range(nhead)]
    same_head = (_iota((width, width), 0) >> 6) == (_iota((width, width), 1) >> 6)
    sel = same_head.astype(BF16)
    chunks = [None] * nchunk

    def by_head(x):
        return jnp.concatenate([x * hmask[h] for h in range(nhead)], axis=0)

    def score_stage(c):
        sl = slice(c * cl, (c + 1) * cl)
        chunks[c] = dict(sl=sl, scores=_dot_nt(q_ref[sl, :], by_head(k_ref[sl, :])))

    def state_stage(c):
        ch = chunks[c]
        sl = ch["sl"]
        st = s_ref[...]
        ch["o"] = _dot((q_ref[sl, :].astype(F32) * qd_ref[...]).astype(BF16), st.astype(BF16))
        kdec = (k_ref[sl, :].astype(F32) * kd_ref[...]).astype(BF16)
        s_ref[...] = st * cd_ref[...] + jnp.where(same_head, _dot_tn(kdec, v_ref[sl, :]), 0.0)

    def value_stage(c):
        ch = chunks[c]
        sc = ch["scores"] * d_ref[...]
        ch["o"] = ch["o"] + _dot(sc.astype(BF16), by_head(v_ref[ch["sl"], :]))

    def norm_stage(c):
        ch = chunks[c]
        o = ch["o"]
        ms = _dot((o * o).astype(BF16), sel) * (1.0 / HEAD_DIM)
        y = o * lax.rsqrt(ms + EPS) * sg_ref[ch["sl"], :].astype(F32)
        o_ref[ch["sl"], :] = y.astype(BF16)
        chunks[c] = None

    stages = [functools.partial(score_stage, 0)]
    for c in range(nchunk):
        if c + 1 < nchunk:
            stages.append(functools.partial(score_stage, c + 1))
        stages.append(functools.partial(state_stage, c))
        if c > 0:
            stages.append(functools.partial(norm_stage, c - 1))
        stages.append(functools.partial(value_stage, c))
    stages.append(functools.partial(norm_stage, nchunk - 1))
    return init, stages


def _swa_stages(sink_ref, q_ref, k_ref, v_ref, kp_ref, vp_ref, o_ref, seq_first, nblk):
    w = WINDOW
    not_first = jnp.logical_not(seq_first)
    low = _iota((2 * w, 128), 1) < HEAD_DIM
    upper = _iota((w, w), 1) > _iota((w, w), 0)
    upper_first = jnp.logical_and(upper, not_first)
    up_b = upper.astype(BF16)
    lo_b = jnp.logical_not(upper).astype(BF16)
    ones = jnp.ones((2 * w, 128), BF16)
    zero = jnp.zeros((2 * w, 128), BF16)
    second_head = _iota((2 * w, 1), 0) >= w
    upper2 = jnp.concatenate([upper, upper], axis=0)
    upper_first2 = jnp.concatenate([upper_first, upper_first], axis=0)
    up_b2 = jnp.concatenate([up_b, up_b], axis=0)
    lo_b2 = jnp.concatenate([lo_b, lo_b], axis=0)

    def operands(blk):
        cur = slice(blk * w, (blk + 1) * w)
        if blk == 0:
            kprev, vprev = kp_ref[...], vp_ref[...]
        else:
            prev = slice((blk - 1) * w, blk * w)
            kprev, vprev = k_ref[prev, :], v_ref[prev, :]
        kk = jnp.concatenate([kprev, k_ref[cur, :]], axis=0)
        vv = jnp.concatenate([vprev, v_ref[cur, :]], axis=0)
        kr = pltpu.roll(kk.astype(F32), HEAD_DIM, axis=1).astype(BF16)
        vr = pltpu.roll(vv.astype(F32), HEAD_DIM, axis=1).astype(BF16)
        ops = dict(cur=cur)
        for g in range(2):
            for half in range(2):
                keep = low if half == 0 else jnp.logical_not(low)
                ops["k", g, half] = jnp.where(keep, kk if g == half else kr, zero)
                ops["v", g, half] = jnp.concatenate([jnp.where(keep, vv if g == half else vr, zero), ones], axis=1)
        return ops

    items = [(blk, g, half) for blk in range(nblk) for g in range(2) for half in range(2)]
    ops_of = {}
    pending = {}
    state = {}

    def score_stage(n):
        blk, g, half = items[n]
        if blk not in ops_of:
            ops_of[blk] = operands(blk)
        cur = ops_of[blk]["cur"]
        q2 = jnp.concatenate([q_ref[cur, 256 * g:256 * g + 128], q_ref[cur, 256 * g + 128:256 * g + 256]], axis=0)
        both = _dot_nt(q2, ops_of[blk]["k", g, half])
        pending[n] = (both[:, :w], both[:, w:])

    def value_stage(n):
        blk, g, half = items[n]
        cur_ops = ops_of[blk]
        s_prev, s_cur = pending.pop(n)
        sink = jnp.where(second_head, sink_ref[4 * g + 2 + half], sink_ref[4 * g + half]) * LOG2E
        if blk == 0:
            s = jnp.where(upper_first2, s_prev, jnp.where(upper2, NEG_BIG, s_cur))
        else:
            s = jnp.where(upper2, s_prev, s_cur)
        m = jnp.maximum(jnp.max(s, axis=-1, keepdims=True), sink)
        p = jnp.exp2(s - m).astype(BF16)
        ol = _dot(jnp.concatenate([p * up_b2, p * lo_b2], axis=1), cur_ops["v", g, half])
        part = ol[:, :128] / (ol[:, 128:] + jnp.exp2(sink - m))
        if half == 0:
            state["acc"] = part
        else:
            both = (state.pop("acc") + part).astype(BF16)
            o_ref[cur_ops["cur"], 256 * g:256 * g + 128] = both[:w]
            o_ref[cur_ops["cur"], 256 * g + 128:256 * g + 256] = both[w:]

    stages = []
    for n in range(len(items) + SWA_LOOKAHEAD):
        if n < len(items):
            stages.append(functools.partial(score_stage, n))
        if n >= SWA_LOOKAHEAD:
            stages.append(functools.partial(value_stage, n - SWA_LOOKAHEAD))
    return stages


def _tail_stages(r_ref, mix_ref, p_ref, wo_ref, gf_ref, wg_ref, wu_ref, cw_ref, cb_ref, wd_ref,
                 gp_ref, wpg_ref, wpp_ref, gfin_ref, o_ref, lhs_ref, ge_ref, act_ref, halo_ref, seq_first, final):
    tm = r_ref.shape[0]
    val = {}

    def head_stage():
        r1 = r_ref[...] + _dot(mix_ref[...], wo_ref[...])
        halo = halo_ref[...]
        lhs_ref[0:CONV_HALO, :] = jnp.where(seq_first, jnp.zeros_like(halo), halo)
        h2 = _rms(r1, gf_ref[...]).astype(BF16)
        lhs_ref[CONV_HALO:, :] = h2
        halo_ref[...] = h2[tm - CONV_HALO:, :]
        val["r1"] = r1

    def mlp_stage(j):
        cols = slice(j * FF_SUB, (j + 1) * FF_SUB)
        ge = ge_ref.at[j % 2]
        ge[...] = _dot(lhs_ref[...], wg_ref[:, cols])
        cw = cw_ref[:, cols]
        gate = (ge[pl.ds(CONV_HALO - 2, tm), :] * cw[0:1, :]
                + ge[pl.ds(CONV_HALO - 1, tm), :] * cw[1:2, :]
                + ge[pl.ds(CONV_HALO, tm), :] * cw[2:3, :]
                + cb_ref[:, cols])
        up = _dot(lhs_ref[CONV_HALO:, :], wu_ref[:, cols])
        act_ref[:, cols] = (jax.nn.gelu(gate, approximate=True) * up).astype(BF16)

    def down_stage(n):
        cols = slice(n * FF_SUB, (n + 1) * FF_SUB)
        val["down", n] = _dot(act_ref[...], wd_ref[:, cols])

    def end_stage():
        r2 = val.pop("r1") + jnp.concatenate([val.pop(("down", n)) for n in range(D_MODEL // FF_SUB)], axis=1)
        gate = _sigmoid(_dot(_rms(r2, gp_ref[...]).astype(BF16), wpg_ref[...]))
        r3 = r2 + _dot(p_ref[...].astype(BF16), wpp_ref[...]) * gate
        o_ref[...] = _rms(r3, gfin_ref[...]) if final else r3

    stages = [head_stage]
    stages += [functools.partial(mlp_stage, j) for j in range(D_FF // FF_SUB)]
    stages += [functools.partial(down_stage, n) for n in range(D_MODEL // FF_SUB)]
    stages.append(end_stage)
    return stages


def _deal(parts):
    keyed = []
    for k, part in enumerate(parts):
        for n, stage in enumerate(part):
            keyed.append(((n + 0.5) / len(part), k, n, stage))
    keyed.sort(key=lambda e: e[:3])
    return [e[3] for e in keyed]


def _body_kernel(sink_ref, lg_ref,
                 hq_ref, hk_ref, hv_ref, hsg_ref, lf_ref, gn_ref,
                 sq_ref, sk_ref, sv_ref, skp_ref, svp_ref,
                 rq_ref, rk_ref, rv_ref, rsg_ref,
                 r_ref, p_ref, wo_ref, gf_ref, wg_ref, wu_ref, cw_ref, cb_ref, wd_ref, gp_ref, wpg_ref, wpp_ref,
                 gfin_ref, o_ref,
                 mixnew_ref, mixold_ref, lhs_ref, ge_ref, act_ref, halo_ref,
                 hst_ref, m3_ref, lev_ref, rst_ref, d_ref, qd_ref, kd_ref, cd_ref,
                 *, ntiles, tiles_per_seq, final):
    tm = r_ref.shape[0]
    i = pl.program_id(0)
    mix_tile = jnp.minimum(i, ntiles - 1)
    mix_first = mix_tile % tiles_per_seq == 0
    tail_first = jnp.maximum(i - 1, 0) % tiles_per_seq == 0

    hgrn_init, hgrn = _hgrn_stages(hq_ref, hk_ref, hv_ref, hsg_ref, lf_ref, gn_ref,
                                   mixnew_ref.at[:, 0:A_WIDTH], hst_ref, m3_ref, lev_ref, tm // HGRN_CHUNK)
    ret_init, ret = _ret_stages(lg_ref, rq_ref, rk_ref, rv_ref, rsg_ref,
                                mixnew_ref.at[:, A_WIDTH + B_WIDTH:], rst_ref, d_ref, qd_ref, kd_ref, cd_ref,
                                tm // RET_CHUNK)
    swa = _swa_stages(sink_ref, sq_ref, sk_ref, sv_ref, skp_ref, svp_ref,
                      mixnew_ref.at[:, A_WIDTH:A_WIDTH + B_WIDTH], mix_first, tm // WINDOW)
    tail = _tail_stages(r_ref, mixold_ref, p_ref, wo_ref, gf_ref, wg_ref, wu_ref, cw_ref, cb_ref, wd_ref,
                        gp_ref, wpg_ref, wpp_ref, gfin_ref, o_ref, lhs_ref, ge_ref, act_ref, halo_ref,
                        tail_first, final)

    @pl.when(i == 0)
    def _():
        mixold_ref[...] = jnp.zeros_like(mixold_ref)
        halo_ref[...] = jnp.zeros_like(halo_ref)

    @pl.when(mix_first)
    def _():
        hgrn_init()
        ret_init()

    for stage in _deal([tail, hgrn, swa, ret]):
        stage()
    mixold_ref[...] = mixnew_ref[...]


def _layer_body(r, zt, lf, p, sinks, log_g, gnorm, w_out, ffn_gain, w_gate, w_up, conv_w, conv_b, w_down,
                ple_gain, w_pg, w_pp, final_gain, layer, seq_len, final):
    t = r.shape[0]
    tm = TOKEN_TILE
    ntiles = t // tm
    per = tm // WINDOW
    once = pl.Buffered(1)
    smem = pl.BlockSpec(memory_space=pltpu.SMEM)

    def mix_tile(i):
        return jnp.minimum(i, ntiles - 1)

    def tail_tile(i):
        return jnp.maximum(i - 1, 0)

    def zcol(width, c0):
        return pl.BlockSpec((tm, width), lambda i: (mix_tile(i), c0 // width))

    def zprev(c0):
        return pl.BlockSpec((WINDOW, B_KV_WIDTH),
                            lambda i: (jnp.maximum(mix_tile(i) * per - 1, 0), c0 // B_KV_WIDTH))

    def whole(shape):
        return pl.BlockSpec((None,) + shape, lambda i: (layer,) + (0,) * len(shape), pipeline_mode=once)

    return pl.pallas_call(
        functools.partial(_body_kernel, ntiles=ntiles, tiles_per_seq=seq_len // tm, final=final),
        grid=(ntiles + 1,),
        in_specs=[smem, smem,
                  zcol(A_WIDTH, AQ), zcol(A_WIDTH, AF), zcol(A_WIDTH, AI), zcol(A_WIDTH, AG),
                  pl.BlockSpec((tm, A_WIDTH), lambda i: (mix_tile(i), 0)),
                  whole((1, A_WIDTH)),
                  zcol(B_WIDTH, BQ), zcol(B_KV_WIDTH, BK), zcol(B_KV_WIDTH, BV), zprev(BK), zprev(BV),
                  zcol(C_WIDTH, CQ), zcol(C_WIDTH, CK), zcol(C_WIDTH, CV), zcol(C_WIDTH, CG),
                  pl.BlockSpec((tm, D_MODEL), lambda i: (tail_tile(i), 0)),
                  pl.BlockSpec((None, tm, PLE_DIM), lambda i: (layer, tail_tile(i), 0)),
                  whole((D_MODEL, D_MODEL)),
                  whole((1, D_MODEL)),
                  whole((D_MODEL, D_FF)),
                  whole((D_MODEL, D_FF)),
                  whole((3, D_FF)),
                  whole((1, D_FF)),
                  whole((D_FF, D_MODEL)),
                  whole((1, D_MODEL)),
                  whole((D_MODEL, D_MODEL)),
                  whole((PLE_DIM, D_MODEL)),
                  pl.BlockSpec((1, D_MODEL), lambda i: (0, 0), pipeline_mode=once)],
        out_specs=pl.BlockSpec((tm, D_MODEL), lambda i: (tail_tile(i), 0)),
        out_shape=jax.ShapeDtypeStruct((t, D_MODEL), F32),
        scratch_shapes=[pltpu.VMEM((tm, D_MODEL), BF16),
                        pltpu.VMEM((tm, D_MODEL), BF16),
                        pltpu.VMEM((tm + CONV_HALO, D_MODEL), BF16),
                        pltpu.VMEM((2, tm + CONV_HALO, FF_SUB), F32),
                        pltpu.VMEM((tm, D_FF), BF16),
                        pltpu.VMEM((CONV_HALO, D_MODEL), BF16),
                        pltpu.VMEM((A_WIDTH, A_WIDTH), F32),
                        pltpu.VMEM((2 * HGRN_CHUNK, 3 * HGRN_CHUNK), BF16),
                        pltpu.VMEM((HGRN_CHUNK, A_WIDTH), jnp.int32),
                        pltpu.VMEM((C_WIDTH, C_WIDTH), F32),
                        pltpu.VMEM((RET_CHUNK, C_WIDTH // HEAD_DIM * RET_CHUNK), F32),
                        pltpu.VMEM((RET_CHUNK, C_WIDTH), F32),
                        pltpu.VMEM((RET_CHUNK, C_WIDTH), F32),
                        pltpu.VMEM((1, C_WIDTH), F32)],
        compiler_params=_params("arbitrary"),
        name="layer_body",
    )(sinks, log_g,
      zt, zt, zt, zt, lf, gnorm,
      zt, zt, zt, zt, zt,
      zt, zt, zt, zt,
      r, p, w_out, ffn_gain, w_gate, w_up, conv_w, conv_b, w_down, ple_gain, w_pg, w_pp, final_gain)


def kernel(x, p, positions, attn_norm, w_in, hgrn_lb, hgrn_gnorm, attn_sinks, w_out, ffn_norm, w_gate, w_up,
           conv_w, conv_b, w_down, ple_norm, w_ple_gate, w_ple_proj, final_norm):
    b, s, d = x.shape
    depth = w_in.shape[0]
    t = b * s
    assert d == D_MODEL and s % TOKEN_TILE == 0 and t % INPROJ_TILE == 0
    assert TOKEN_TILE % RET_CHUNK == 0 and TOKEN_TILE % HGRN_CHUNK == 0 and TOKEN_TILE % WINDOW == 0

    vec = lambda a: a.reshape(depth, 1, a.shape[-1])
    bf = lambda a: a.astype(BF16)

    lb_all = pl.pallas_call(
        _lower_bound_kernel, out_shape=jax.ShapeDtypeStruct(hgrn_lb.shape, F32), name="hgrn_lower_bound",
    )(hgrn_lb.astype(F32))

    inv = 1.0 / (ROPE_THETA ** (jnp.arange(0, HEAD_DIM, 2, dtype=F32) / HEAD_DIM))
    inv = jnp.tile(inv, 4).reshape(1, 128)
    tt = INPROJ_TILE
    cos, sin_a, sin_b = pl.pallas_call(
        _rope_table_kernel,
        grid=(t // tt,),
        in_specs=[pl.BlockSpec((tt, 1), lambda i: (i, 0)), pl.BlockSpec((1, 128), lambda i: (0, 0))],
        out_specs=[pl.BlockSpec((tt, 128), lambda i: (i, 0))] * 3,
        out_shape=[jax.ShapeDtypeStruct((t, 128), F32)] * 3,
        compiler_params=_params("parallel"),
        name="rope_tables",
    )(positions.reshape(t, 1), inv)

    nhead_c = C_WIDTH // HEAD_DIM
    log_g = jnp.log(1.0 - 2.0 ** (-5.0 - jnp.arange(nhead_c, dtype=F32)))

    w_in_b, w_out_b, w_gate_b, w_up_b, w_down_b = bf(w_in), bf(w_out), bf(w_gate), bf(w_up), bf(w_down)
    w_pg_b, w_pp_b = bf(w_ple_gate), bf(w_ple_proj)
    attn_norm_v, ffn_norm_v, ple_norm_v = vec(attn_norm), vec(ffn_norm), vec(ple_norm)
    gnorm_v, conv_b_v, lb_v = vec(hgrn_gnorm), vec(conv_b), vec(lb_all)
    p_flat = p.reshape(depth, t, PLE_DIM)

    r = x.reshape(t, d).astype(F32)
    for i in range(depth):
        zt, lf = _inproj(r, attn_norm_v, w_in_b, cos, sin_a, sin_b, lb_v, i)
        r = _layer_body(r, zt, lf, p_flat, attn_sinks[i].astype(F32), log_g, gnorm_v, w_out_b, ffn_norm_v,
                        w_gate_b, w_up_b, conv_w, conv_b_v, w_down_b, ple_norm_v, w_pg_b, w_pp_b,
                        final_norm.reshape(1, d), i, s, i == depth - 1)
    return r.reshape(b, s, d).astype(x.dtype)
```

```python
import functools

import jax
import jax.numpy as jnp
from jax import lax
from jax.experimental import pallas as pl
from jax.experimental.pallas import tpu as pltpu

F32 = jnp.float32
BF16 = jnp.bfloat16

D_MODEL = 1024
HEAD_DIM = 64
A_WIDTH = 256
B_WIDTH = 512
B_KV_WIDTH = 128
C_WIDTH = 256
WINDOW = 128
D_IN = 2816
D_FF = 2816
PLE_DIM = 256
ROPE_THETA = 10000.0
EPS = 1e-6
NEG_BIG = -1e30
LOG2E = 1.4426950408889634

AQ, AF, AI, AG = 0, 256, 512, 768
BQ, BK, BV = 1024, 1536, 1664
CQ, CK, CV, CG = 1792, 2048, 2304, 2560

TOKEN_TILE = 512
INPROJ_TILE = 1024
SWA_LOOKAHEAD = 2
HGRN_CHUNK = 64
RET_CHUNK = 128
FF_SUB = 256
CONV_HALO = 16
VMEM_LIMIT = 56 * 1024 * 1024


def _dot(a, b):
    return jnp.dot(a, b, preferred_element_type=F32)


def _dot_nt(a, b):
    return lax.dot_general(a, b, (((1,), (1,)), ((), ())), preferred_element_type=F32)


def _dot_tn(a, b):
    return lax.dot_general(a, b, (((0,), (0,)), ((), ())), preferred_element_type=F32)


def _iota(shape, dim):
    return lax.broadcasted_iota(jnp.int32, shape, dim)


def _sigmoid(x):
    return 1.0 / (1.0 + jnp.exp(-x))


def _rms(x, gain):
    y = x * lax.rsqrt(jnp.mean(x * x, axis=-1, keepdims=True) + EPS)
    return y * gain


def _params(*sem):
    return pltpu.CompilerParams(dimension_semantics=sem, vmem_limit_bytes=VMEM_LIMIT)


def _lower_bound_kernel(lb_ref, o_ref):
    x = lb_ref[...]
    depth = x.shape[0]
    e = jnp.exp(x - jnp.max(x, axis=0, keepdims=True))
    p = e / jnp.sum(e, axis=0, keepdims=True)
    row = _iota(x.shape, 0)
    acc = jnp.zeros_like(x)
    for j in range(depth):
        acc = acc + jnp.where(row >= j, p[j:j + 1, :], 0.0)
    o_ref[...] = acc - p[0:1, :]


def _rope_table_kernel(pos_ref, inv_ref, cos_ref, sa_ref, sb_ref):
    ang = pos_ref[...].astype(F32) * inv_ref[...]
    c = jnp.cos(ang)
    s = jnp.sin(ang)
    first_half = (_iota(ang.shape, 1) & (HEAD_DIM - 1)) < HEAD_DIM // 2
    cos_ref[...] = c
    sa_ref[...] = jnp.where(first_half, -s, 0.0)
    sb_ref[...] = jnp.where(first_half, 0.0, s)


def _inproj_kernel(x_ref, g_ref, w_ref, cos_ref, sa_ref, sb_ref, lb_ref, zt_ref, lf_ref):
    h = _rms(x_ref[...], g_ref[...]).astype(BF16)
    cos = cos_ref[...]
    sa = sa_ref[...]
    sb = sb_ref[...]

    def proj(c0, width):
        return _dot(h, w_ref[:, c0:c0 + width])

    def rope(z):
        return z * cos + pltpu.roll(z, 96, axis=1) * sa + pltpu.roll(z, 32, axis=1) * sb

    def put(c0, val):
        zt_ref[:, c0:c0 + val.shape[1]] = val.astype(BF16)

    def put_rope(c0, z, scale):
        for j in range(z.shape[1] // 128):
            put(c0 + 128 * j, rope(z[:, 128 * j:128 * (j + 1)]) * scale)

    put(AQ, proj(AQ, A_WIDTH))
    fl = proj(AF, A_WIDTH)
    e = jnp.exp(-jnp.abs(fl))
    inv = 1.0 / (1.0 + e)
    sig_pos = jnp.where(fl >= 0, inv, e * inv)
    sig_neg = jnp.where(fl >= 0, e * inv, inv)
    lb = lb_ref[...]
    lf_ref[...] = jnp.log(lb + (1.0 - lb) * sig_pos) * LOG2E
    put(AF, (1.0 - lb) * sig_neg)
    put(AI, proj(AI, A_WIDTH))
    g = proj(AG, A_WIDTH)
    put(AG, g * _sigmoid(g))
    put_rope(BQ, proj(BQ, B_WIDTH), HEAD_DIM ** -0.5 * LOG2E)
    kv = proj(BK, 2 * B_KV_WIDTH)
    put_rope(BK, kv[:, :B_KV_WIDTH], 1.0)
    put(BV, kv[:, B_KV_WIDTH:])
    put_rope(CQ, proj(CQ, C_WIDTH), 1.0)
    put_rope(CK, proj(CK, C_WIDTH), HEAD_DIM ** -0.5)
    put(CV, proj(CV, C_WIDTH))
    g = proj(CG, C_WIDTH)
    put(CG, g * _sigmoid(g))


def _inproj(r, gain, w_in, cos, sa, sb, lb_all, layer):
    t = r.shape[0]
    tm = INPROJ_TILE
    row = lambda i: (i, 0)
    return pl.pallas_call(
        _inproj_kernel,
        grid=(t // tm,),
        in_specs=[
            pl.BlockSpec((tm, D_MODEL), row),
            pl.BlockSpec((None, 1, D_MODEL), lambda i: (layer, 0, 0)),
            pl.BlockSpec((None, D_MODEL, D_IN), lambda i: (layer, 0, 0)),
            pl.BlockSpec((tm, 128), row),
            pl.BlockSpec((tm, 128), row),
            pl.BlockSpec((tm, 128), row),
            pl.BlockSpec((None, 1, A_WIDTH), lambda i: (layer, 0, 0)),
        ],
        out_specs=[pl.BlockSpec((tm, D_IN), row), pl.BlockSpec((tm, A_WIDTH), row)],
        out_shape=[jax.ShapeDtypeStruct((t, D_IN), BF16), jax.ShapeDtypeStruct((t, A_WIDTH), F32)],
        compiler_params=_params("parallel"),
        name="inproj",
    )(r, gain, w_in, cos, sa, sb, lb_all)


def _hgrn_stages(q_ref, k_ref, v_ref, sg_ref, lf_ref, gn_ref, o_ref, st_ref, m3_ref, lev_ref, nchunk):
    width = A_WIDTH
    cl = HGRN_CHUNK
    nlev = cl.bit_length() - 1
    nhead = width // HEAD_DIM

    def init():
        st_ref[...] = jnp.zeros_like(st_ref)
        t = _iota((cl, 3 * cl), 0)
        s = _iota((cl, 3 * cl), 1) & (cl - 1)
        m3_ref[0:cl, :] = (s <= t).astype(BF16)
        m3_ref[cl:2 * cl, :] = (s <= (t & -4) + 1).astype(BF16)
        tq = _iota((cl, width), 0)
        sk = _iota((cl, width), 1) & (cl - 1)
        diff = tq ^ sk
        lvl = jnp.zeros((cl, width), jnp.int32)
        for j in range(1, nlev):
            lvl = lvl + (diff >= (1 << j)).astype(jnp.int32)
        lev_ref[...] = jnp.where(tq > sk, lvl, jnp.where(tq == sk, -2, -1))

    row = _iota((cl, width), 0)
    head = _iota((cl, width), 1) >> 6
    hmask = [(head == h).astype(BF16) for h in range(nhead)]
    same_head = (_iota((width, width), 0) >> 6) == (_iota((width, width), 1) >> 6)
    sel = same_head.astype(BF16)
    gain = gn_ref[...]
    chunks = [None] * nchunk

    def by_head(x):
        return jnp.concatenate([x * hmask[h] for h in range(nhead)], axis=0)

    def by_head_t(x):
        return jnp.concatenate([x] * nhead, axis=0).T * sel

    def sums_stage(c):
        sl = slice(c * cl, (c + 1) * cl)
        lf = lf_ref[sl, :]
        hi = lf.astype(BF16)
        rem = lf - hi.astype(F32)
        mid = rem.astype(BF16)
        lo = (rem - mid.astype(F32)).astype(BF16)
        chunks[c] = dict(sl=sl, lf=lf, sums=_dot(m3_ref[...], jnp.concatenate([hi, mid, lo], axis=0)))

    def level_stage(c):
        ch = chunks[c]
        sl, lf, sums = ch["sl"], ch["lf"], ch["sums"]
        qf = q_ref[sl, :].astype(F32)
        kf = k_ref[sl, :].astype(F32)
        bc = sums[0:cl]
        bl = bc[cl - 1:cl, :]
        scores = []
        for l in range(nlev):
            m = 1 << l
            second = (row & m) != 0
            if l == 0:
                e = jnp.where(second, jnp.exp2(lf), 1.0)
            else:
                if l == 1:
                    ref = sums[cl:2 * cl]
                else:
                    ref = jnp.concatenate(
                        [jnp.broadcast_to(bc[g + m - 1:g + m, :], (2 * m, width)) for g in range(0, cl, 2 * m)], axis=0)
                e = jnp.exp2(-jnp.abs(bc - ref))
            x = (jnp.where(second, qf, kf) * e).astype(BF16)
            scores.append(_dot(x, by_head_t(x)))
        diag = _dot((qf * kf).astype(BF16), sel)
        lev = lev_ref[...]
        sc = jnp.zeros((cl, width), F32)
        for l in range(nlev):
            sc = jnp.where(lev == l, scores[l], sc)
        sc = jnp.where(lev == -2, diag, sc)
        ch.update(sc=sc.astype(BF16),
                  qe=(qf * jnp.exp2(bc)).astype(BF16), kd=(kf * jnp.exp2(bl - bc)).astype(BF16), ebl=jnp.exp2(bl))

    def state_stage(c):
        ch = chunks[c]
        vb = v_ref[ch["sl"], :]
        st = st_ref[...]
        ch["o"] = _dot_nt(ch["qe"], st.astype(BF16))
        st_ref[...] = st * ch["ebl"] + jnp.where(same_head, _dot_tn(vb, ch["kd"]), 0.0)

    def value_stage(c):
        ch = chunks[c]
        ch["o"] = ch["o"] + _dot(ch.pop("sc"), by_head(v_ref[ch["sl"], :]))

    def norm_stage(c):
        ch = chunks[c]
        o = ch["o"]
        ms = _dot((o * o).astype(BF16), sel) * (1.0 / HEAD_DIM)
        y = o * lax.rsqrt(ms + EPS) * gain * sg_ref[ch["sl"], :].astype(F32)
        o_ref[ch["sl"], :] = y.astype(BF16)
        chunks[c] = None

    stages = [functools.partial(sums_stage, 0)]
    for c in range(nchunk):
        if c + 1 < nchunk:
            stages.append(functools.partial(sums_stage, c + 1))
        stages.append(functools.partial(level_stage, c))
        stages.append(functools.partial(state_stage, c))
        if c > 0:
            stages.append(functools.partial(norm_stage, c - 1))
        stages.append(functools.partial(value_stage, c))
    stages.append(functools.partial(norm_stage, nchunk - 1))
    return init, stages


def _ret_stages(lg_ref, q_ref, k_ref, v_ref, sg_ref, o_ref, s_ref, d_ref, qd_ref, kd_ref, cd_ref, nchunk):
    width = C_WIDTH
    cl = RET_CHUNK
    nhead = width // HEAD_DIM
    head_row = _iota((1, width), 1) >> 6

    def init():
        s_ref[...] = jnp.zeros_like(s_ref)
        lgl = jnp.zeros((1, width), F32)
        rel = _iota((cl, cl), 0) - _iota((cl, cl), 1)
        causal = rel >= 0
        relf = jnp.where(causal, rel, 0).astype(F32)
        for h in range(nhead):
            lgl = jnp.where(head_row == h, lg_ref[h], lgl)
            d_ref[:, cl * h:cl * (h + 1)] = jnp.where(causal, jnp.exp(relf * lg_ref[h]), 0.0)
        t = _iota((cl, width), 0).astype(F32)
        qd_ref[...] = jnp.exp((t + 1.0) * lgl)
        kd_ref[...] = jnp.exp((cl - 1.0 - t) * lgl)
        cd_ref[...] = jnp.exp(float(cl) * lgl)

    head = _iota((cl, width), 1) >> 6
    hmask = [(head == h).astype(BF16) for h in range(nhead)]
    same_head = (_iota((width, width), 0) >> 6) == (_iota((width, width), 1) >> 6)
    sel = same_head.astype(BF16)
    sel_t = ((_iota((width, nhead * cl), 0) >> 6)
             == (_iota((width, nhead * cl), 1) >> (cl.bit_length() - 1))).astype(BF16)
    chunks = [None] * nchunk

    def by_head(x):
        return jnp.concatenate([x * hmask[h] for h in range(nhead)], axis=0)

    def score_stage(c):
        sl = slice(c * cl, (c + 1) * cl)
        k4t = jnp.concatenate([k_ref[sl, :]] * nhead, axis=0).T
        chunks[c] = dict(sl=sl, scores=(_dot(q_ref[sl, :], k4t * sel_t) * d_ref[...]).astype(BF16))

    def state_stage(c):
        ch = chunks[c]
        sl = ch["sl"]
        st = s_ref[...]
        ch["o"] = _dot((q_ref[sl, :].astype(F32) * qd_ref[...]).astype(BF16), st.astype(BF16))
        kdec = (k_ref[sl, :].astype(F32) * kd_ref[...]).astype(BF16)
        s_ref[...] = st * cd_ref[...] + jnp.where(same_head, _dot_tn(kdec, v_ref[sl, :]), 0.0)

    def value_stage(c):
        ch = chunks[c]
        ch["o"] = ch["o"] + _dot(ch.pop("scores"), by_head(v_ref[ch["sl"], :]))

    def norm_stage(c):
        ch = chunks[c]
        o = ch["o"]
        ms = _dot((o * o).astype(BF16), sel) * (1.0 / HEAD_DIM)
        y = o * lax.rsqrt(ms + EPS) * sg_ref[ch["sl"], :].astype(F32)
        o_ref[ch["sl"], :] = y.astype(BF16)
        chunks[c] = None

    stages = [functools.partial(score_stage, 0)]
    for c in range(nchunk):
        if c + 1 < nchunk:
            stages.append(functools.partial(score_stage, c + 1))
        stages.append(functools.partial(state_stage, c))
        if c > 0:
            stages.append(functools.partial(norm_stage, c - 1))
        stages.append(functools.partial(value_stage, c))
    stages.append(functools.partial(norm_stage, nchunk - 1))
    return init, stages


def _swa_stages(sink_ref, q_ref, k_ref, v_ref, kp_ref, vp_ref, o_ref, seq_first, nblk):
    w = WINDOW
    not_first = jnp.logical_not(seq_first)
    low = _iota((2 * w, 128), 1) < HEAD_DIM
    upper = _iota((w, w), 1) > _iota((w, w), 0)
    upper_first = jnp.logical_and(upper, not_first)
    up_b = upper.astype(BF16)
    lo_b = jnp.logical_not(upper).astype(BF16)
    ones = jnp.ones((2 * w, 128), BF16)
    zero = jnp.zeros((2 * w, 128), BF16)
    second_head = _iota((2 * w, 1), 0) >= w
    upper2 = jnp.concatenate([upper, upper], axis=0)
    upper_first2 = jnp.concatenate([upper_first, upper_first], axis=0)
    up_b2 = jnp.concatenate([up_b, up_b], axis=0)
    lo_b2 = jnp.concatenate([lo_b, lo_b], axis=0)

    def operands(blk):
        cur = slice(blk * w, (blk + 1) * w)
        if blk == 0:
            kprev, vprev = kp_ref[...], vp_ref[...]
        else:
            prev = slice((blk - 1) * w, blk * w)
            kprev, vprev = k_ref[prev, :], v_ref[prev, :]
        kk = jnp.concatenate([kprev, k_ref[cur, :]], axis=0)
        vv = jnp.concatenate([vprev, v_ref[cur, :]], axis=0)
        kr = pltpu.roll(kk.astype(F32), HEAD_DIM, axis=1).astype(BF16)
        vr = pltpu.roll(vv.astype(F32), HEAD_DIM, axis=1).astype(BF16)
        ops = dict(cur=cur)
        for g in range(2):
            for half in range(2):
                keep = low if half == 0 else jnp.logical_not(low)
                ops["kt", g, half] = jnp.where(keep, kk if g == half else kr, zero).T
                ops["v", g, half] = jnp.concatenate([jnp.where(keep, vv if g == half else vr, zero), ones], axis=1)
        return ops

    items = [(blk, g, half) for blk in range(nblk) for g in range(2) for half in range(2)]
    ops_of = {}
    pending = {}
    state = {}

    def score_stage(n):
        blk, g, half = items[n]
        if blk not in ops_of:
            ops_of[blk] = operands(blk)
        cur = ops_of[blk]["cur"]
        q2 = jnp.concatenate([q_ref[cur, 256 * g:256 * g + 128], q_ref[cur, 256 * g + 128:256 * g + 256]], axis=0)
        both = _dot(q2, ops_of[blk]["kt", g, half])
        s_prev, s_cur = both[:, :w], both[:, w:]
        sink = jnp.where(second_head, sink_ref[4 * g + 2 + half], sink_ref[4 * g + half]) * LOG2E
        if blk == 0:
            s = jnp.where(upper_first2, s_prev, jnp.where(upper2, NEG_BIG, s_cur))
        else:
            s = jnp.where(upper2, s_prev, s_cur)
        m = jnp.maximum(jnp.max(s, axis=-1, keepdims=True), sink)
        p = jnp.exp2(s - m).astype(BF16)
        pending[n] = (jnp.concatenate([p * up_b2, p * lo_b2], axis=1), jnp.exp2(sink - m))

    def value_stage(n):
        blk, g, half = items[n]
        cur_ops = ops_of[blk]
        probs, sink_term = pending.pop(n)
        ol = _dot(probs, cur_ops["v", g, half])
        part = ol[:, :128] / (ol[:, 128:] + sink_term)
        if half == 0:
            state["acc"] = part
        else:
            both = (state.pop("acc") + part).astype(BF16)
            o_ref[cur_ops["cur"], 256 * g:256 * g + 128] = both[:w]
            o_ref[cur_ops["cur"], 256 * g + 128:256 * g + 256] = both[w:]

    stages = []
    for n in range(len(items) + SWA_LOOKAHEAD):
        if n < len(items):
            stages.append(functools.partial(score_stage, n))
        if n >= SWA_LOOKAHEAD:
            stages.append(functools.partial(value_stage, n - SWA_LOOKAHEAD))
    return stages


def _tail_stages(r_ref, mix_ref, p_ref, wo_ref, gf_ref, wg_ref, wu_ref, cw_ref, cb_ref, wd_ref,
                 gp_ref, wpg_ref, wpp_ref, gfin_ref, o_ref, lhs_ref, ge_ref, act_ref, halo_ref, seq_first, final):
    tm = r_ref.shape[0]
    val = {}

    def head_stage():
        r1 = r_ref[...] + _dot(mix_ref[...], wo_ref[...])
        halo = halo_ref[...]
        lhs_ref[0:CONV_HALO, :] = jnp.where(seq_first, jnp.zeros_like(halo), halo)
        h2 = _rms(r1, gf_ref[...]).astype(BF16)
        lhs_ref[CONV_HALO:, :] = h2
        halo_ref[...] = h2[tm - CONV_HALO:, :]
        val["r1"] = r1

    def mlp_stage(j):
        cols = slice(j * FF_SUB, (j + 1) * FF_SUB)
        ge = ge_ref.at[j % 2]
        ge[...] = _dot(lhs_ref[...], wg_ref[:, cols])
        cw = cw_ref[:, cols]
        gate = (ge[pl.ds(CONV_HALO - 2, tm), :] * cw[0:1, :]
                + ge[pl.ds(CONV_HALO - 1, tm), :] * cw[1:2, :]
                + ge[pl.ds(CONV_HALO, tm), :] * cw[2:3, :]
                + cb_ref[:, cols])
        up = _dot(lhs_ref[CONV_HALO:, :], wu_ref[:, cols])
        act_ref[:, cols] = (jax.nn.gelu(gate, approximate=True) * up).astype(BF16)

    def down_stage(n):
        cols = slice(n * FF_SUB, (n + 1) * FF_SUB)
        val["down", n] = _dot(act_ref[...], wd_ref[:, cols])

    def end_stage():
        r2 = val.pop("r1") + jnp.concatenate([val.pop(("down", n)) for n in range(D_MODEL // FF_SUB)], axis=1)
        gate = _sigmoid(_dot(_rms(r2, gp_ref[...]).astype(BF16), wpg_ref[...]))
        r3 = r2 + _dot(p_ref[...].astype(BF16), wpp_ref[...]) * gate
        o_ref[...] = _rms(r3, gfin_ref[...]) if final else r3

    stages = [head_stage]
    stages += [functools.partial(mlp_stage, j) for j in range(D_FF // FF_SUB)]
    stages += [functools.partial(down_stage, n) for n in range(D_MODEL // FF_SUB)]
    stages.append(end_stage)
    return stages


def _deal(parts):
    keyed = []
    for k, part in enumerate(parts):
        for n, stage in enumerate(part):
            keyed.append(((n + 0.5) / len(part), k, n, stage))
    keyed.sort(key=lambda e: e[:3])
    return [e[3] for e in keyed]


def _body_kernel(sink_ref, lg_ref,
                 hq_ref, hk_ref, hv_ref, hsg_ref, lf_ref, gn_ref,
                 sq_ref, sk_ref, sv_ref, skp_ref, svp_ref,
                 rq_ref, rk_ref, rv_ref, rsg_ref,
                 r_ref, p_ref, wo_ref, gf_ref, wg_ref, wu_ref, cw_ref, cb_ref, wd_ref, gp_ref, wpg_ref, wpp_ref,
                 gfin_ref, o_ref,
                 mixnew_ref, mixold_ref, lhs_ref, ge_ref, act_ref, halo_ref,
                 hst_ref, m3_ref, lev_ref, rst_ref, d_ref, qd_ref, kd_ref, cd_ref,
                 *, ntiles, tiles_per_seq, final):
    tm = r_ref.shape[0]
    i = pl.program_id(0)
    mix_tile = jnp.minimum(i, ntiles - 1)
    mix_first = mix_tile % tiles_per_seq == 0
    tail_first = jnp.maximum(i - 1, 0) % tiles_per_seq == 0

    hgrn_init, hgrn = _hgrn_stages(hq_ref, hk_ref, hv_ref, hsg_ref, lf_ref, gn_ref,
                                   mixnew_ref.at[:, 0:A_WIDTH], hst_ref, m3_ref, lev_ref, tm // HGRN_CHUNK)
    ret_init, ret = _ret_stages(lg_ref, rq_ref, rk_ref, rv_ref, rsg_ref,
                                mixnew_ref.at[:, A_WIDTH + B_WIDTH:], rst_ref, d_ref, qd_ref, kd_ref, cd_ref,
                                tm // RET_CHUNK)
    swa = _swa_stages(sink_ref, sq_ref, sk_ref, sv_ref, skp_ref, svp_ref,
                      mixnew_ref.at[:, A_WIDTH:A_WIDTH + B_WIDTH], mix_first, tm // WINDOW)
    tail = _tail_stages(r_ref, mixold_ref, p_ref, wo_ref, gf_ref, wg_ref, wu_ref, cw_ref, cb_ref, wd_ref,
                        gp_ref, wpg_ref, wpp_ref, gfin_ref, o_ref, lhs_ref, ge_ref, act_ref, halo_ref,
                        tail_first, final)

    @pl.when(i == 0)
    def _():
        mixold_ref[...] = jnp.zeros_like(mixold_ref)
        halo_ref[...] = jnp.zeros_like(halo_ref)

    @pl.when(mix_first)
    def _():
        hgrn_init()
        ret_init()

    for stage in _deal([tail, hgrn, swa, ret]):
        stage()
    mixold_ref[...] = mixnew_ref[...]


def _layer_body(r, zt, lf, p, sinks, log_g, gnorm, w_out, ffn_gain, w_gate, w_up, conv_w, conv_b, w_down,
                ple_gain, w_pg, w_pp, final_gain, layer, seq_len, final):
    t = r.shape[0]
    tm = TOKEN_TILE
    ntiles = t // tm
    per = tm // WINDOW
    once = pl.Buffered(1)
    smem = pl.BlockSpec(memory_space=pltpu.SMEM)

    def mix_tile(i):
        return jnp.minimum(i, ntiles - 1)

    def tail_tile(i):
        return jnp.maximum(i - 1, 0)

    def zcol(width, c0):
        return pl.BlockSpec((tm, width), lambda i: (mix_tile(i), c0 // width))

    def zprev(c0):
        return pl.BlockSpec((WINDOW, B_KV_WIDTH),
                            lambda i: (jnp.maximum(mix_tile(i) * per - 1, 0), c0 // B_KV_WIDTH))

    def whole(shape):
        return pl.BlockSpec((None,) + shape, lambda i: (layer,) + (0,) * len(shape), pipeline_mode=once)

    return pl.pallas_call(
        functools.partial(_body_kernel, ntiles=ntiles, tiles_per_seq=seq_len // tm, final=final),
        grid=(ntiles + 1,),
        in_specs=[smem, smem,
                  zcol(A_WIDTH, AQ), zcol(A_WIDTH, AF), zcol(A_WIDTH, AI), zcol(A_WIDTH, AG),
                  pl.BlockSpec((tm, A_WIDTH), lambda i: (mix_tile(i), 0)),
                  whole((1, A_WIDTH)),
                  zcol(B_WIDTH, BQ), zcol(B_KV_WIDTH, BK), zcol(B_KV_WIDTH, BV), zprev(BK), zprev(BV),
                  zcol(C_WIDTH, CQ), zcol(C_WIDTH, CK), zcol(C_WIDTH, CV), zcol(C_WIDTH, CG),
                  pl.BlockSpec((tm, D_MODEL), lambda i: (tail_tile(i), 0)),
                  pl.BlockSpec((None, tm, PLE_DIM), lambda i: (layer, tail_tile(i), 0)),
                  whole((D_MODEL, D_MODEL)),
                  whole((1, D_MODEL)),
                  whole((D_MODEL, D_FF)),
                  whole((D_MODEL, D_FF)),
                  whole((3, D_FF)),
                  whole((1, D_FF)),
                  whole((D_FF, D_MODEL)),
                  whole((1, D_MODEL)),
                  whole((D_MODEL, D_MODEL)),
                  whole((PLE_DIM, D_MODEL)),
                  pl.BlockSpec((1, D_MODEL), lambda i: (0, 0), pipeline_mode=once)],
        out_specs=pl.BlockSpec((tm, D_MODEL), lambda i: (tail_tile(i), 0)),
        out_shape=jax.ShapeDtypeStruct((t, D_MODEL), F32),
        scratch_shapes=[pltpu.VMEM((tm, D_MODEL), BF16),
                        pltpu.VMEM((tm, D_MODEL), BF16),
                        pltpu.VMEM((tm + CONV_HALO, D_MODEL), BF16),
                        pltpu.VMEM((2, tm + CONV_HALO, FF_SUB), F32),
                        pltpu.VMEM((tm, D_FF), BF16),
                        pltpu.VMEM((CONV_HALO, D_MODEL), BF16),
                        pltpu.VMEM((A_WIDTH, A_WIDTH), F32),
                        pltpu.VMEM((2 * HGRN_CHUNK, 3 * HGRN_CHUNK), BF16),
                        pltpu.VMEM((HGRN_CHUNK, A_WIDTH), jnp.int32),
                        pltpu.VMEM((C_WIDTH, C_WIDTH), F32),
                        pltpu.VMEM((RET_CHUNK, C_WIDTH // HEAD_DIM * RET_CHUNK), F32),
                        pltpu.VMEM((RET_CHUNK, C_WIDTH), F32),
                        pltpu.VMEM((RET_CHUNK, C_WIDTH), F32),
                        pltpu.VMEM((1, C_WIDTH), F32)],
        compiler_params=_params("arbitrary"),
        name="layer_body",
    )(sinks, log_g,
      zt, zt, zt, zt, lf, gnorm,
      zt, zt, zt, zt, zt,
      zt, zt, zt, zt,
      r, p, w_out, ffn_gain, w_gate, w_up, conv_w, conv_b, w_down, ple_gain, w_pg, w_pp, final_gain)


def kernel(x, p, positions, attn_norm, w_in, hgrn_lb, hgrn_gnorm, attn_sinks, w_out, ffn_norm, w_gate, w_up,
           conv_w, conv_b, w_down, ple_norm, w_ple_gate, w_ple_proj, final_norm):
    b, s, d = x.shape
    depth = w_in.shape[0]
    t = b * s
    assert d == D_MODEL and s % TOKEN_TILE == 0 and t % INPROJ_TILE == 0
    assert TOKEN_TILE % RET_CHUNK == 0 and TOKEN_TILE % HGRN_CHUNK == 0 and TOKEN_TILE % WINDOW == 0

    vec = lambda a: a.reshape(depth, 1, a.shape[-1])
    bf = lambda a: a.astype(BF16)

    lb_all = pl.pallas_call(
        _lower_bound_kernel, out_shape=jax.ShapeDtypeStruct(hgrn_lb.shape, F32), name="hgrn_lower_bound",
    )(hgrn_lb.astype(F32))

    inv = 1.0 / (ROPE_THETA ** (jnp.arange(0, HEAD_DIM, 2, dtype=F32) / HEAD_DIM))
    inv = jnp.tile(inv, 4).reshape(1, 128)
    tt = INPROJ_TILE
    cos, sin_a, sin_b = pl.pallas_call(
        _rope_table_kernel,
        grid=(t // tt,),
        in_specs=[pl.BlockSpec((tt, 1), lambda i: (i, 0)), pl.BlockSpec((1, 128), lambda i: (0, 0))],
        out_specs=[pl.BlockSpec((tt, 128), lambda i: (i, 0))] * 3,
        out_shape=[jax.ShapeDtypeStruct((t, 128), F32)] * 3,
        compiler_params=_params("parallel"),
        name="rope_tables",
    )(positions.reshape(t, 1), inv)

    nhead_c = C_WIDTH // HEAD_DIM
    log_g = jnp.log(1.0 - 2.0 ** (-5.0 - jnp.arange(nhead_c, dtype=F32)))

    w_in_b, w_out_b, w_gate_b, w_up_b, w_down_b = bf(w_in), bf(w_out), bf(w_gate), bf(w_up), bf(w_down)
    w_pg_b, w_pp_b = bf(w_ple_gate), bf(w_ple_proj)
    attn_norm_v, ffn_norm_v, ple_norm_v = vec(attn_norm), vec(ffn_norm), vec(ple_norm)
    gnorm_v, conv_b_v, lb_v = vec(hgrn_gnorm), vec(conv_b), vec(lb_all)
    p_flat = p.reshape(depth, t, PLE_DIM)

    r = x.reshape(t, d).astype(F32)
    for i in range(depth):
        zt, lf = _inproj(r, attn_norm_v, w_in_b, cos, sin_a, sin_b, lb_v, i)
        r = _layer_body(r, zt, lf, p_flat, attn_sinks[i].astype(F32), log_g, gnorm_v, w_out_b, ffn_norm_v,
                        w_gate_b, w_up_b, conv_w, conv_b_v, w_down_b, ple_norm_v, w_pg_b, w_pp_b,
                        final_norm.reshape(1, d), i, s, i == depth - 1)
    return r.reshape(b, s, d).astype(x.dtype)
```

```python
import functools

import jax
import jax.numpy as jnp
from jax import lax
from jax.experimental import pallas as pl
from jax.experimental.pallas import tpu as pltpu

F32 = jnp.float32
BF16 = jnp.bfloat16

D_MODEL = 1024
HEAD_DIM = 64
A_WIDTH = 256
B_WIDTH = 512
B_KV_WIDTH = 128
C_WIDTH = 256
WINDOW = 128
D_IN = 2816
D_FF = 2816
PLE_DIM = 256
ROPE_THETA = 10000.0
EPS = 1e-6
NEG_BIG = -1e30
LOG2E = 1.4426950408889634

AQ, AF, AI, AG = 0, 256, 512, 768
BQ, BK, BV = 1024, 1536, 1664
CQ, CK, CV, CG = 1792, 2048, 2304, 2560

TOKEN_TILE = 512
INPROJ_TILE = 1024
SWA_LOOKAHEAD = 2
HGRN_CHUNK = 64
RET_CHUNK = 128
FF_SUB = 256
CONV_HALO = 16
VMEM_LIMIT = 56 * 1024 * 1024


def _dot(a, b):
    return jnp.dot(a, b, preferred_element_type=F32)


def _dot_nt(a, b):
    return lax.dot_general(a, b, (((1,), (1,)), ((), ())), preferred_element_type=F32)


def _dot_tn(a, b):
    return lax.dot_general(a, b, (((0,), (0,)), ((), ())), preferred_element_type=F32)


def _iota(shape, dim):
    return lax.broadcasted_iota(jnp.int32, shape, dim)


def _sigmoid(x):
    return 1.0 / (1.0 + jnp.exp(-x))


def _rms(x, gain):
    y = x * lax.rsqrt(jnp.mean(x * x, axis=-1, keepdims=True) + EPS)
    return y * gain


def _params(*sem):
    return pltpu.CompilerParams(dimension_semantics=sem, vmem_limit_bytes=VMEM_LIMIT)


def _lower_bound_kernel(lb_ref, o_ref):
    x = lb_ref[...]
    depth = x.shape[0]
    e = jnp.exp(x - jnp.max(x, axis=0, keepdims=True))
    p = e / jnp.sum(e, axis=0, keepdims=True)
    row = _iota(x.shape, 0)
    acc = jnp.zeros_like(x)
    for j in range(depth):
        acc = acc + jnp.where(row >= j, p[j:j + 1, :], 0.0)
    o_ref[...] = acc - p[0:1, :]


def _rope_table_kernel(pos_ref, inv_ref, cos_ref, sa_ref, sb_ref):
    ang = pos_ref[...].astype(F32) * inv_ref[...]
    c = jnp.cos(ang)
    s = jnp.sin(ang)
    first_half = (_iota(ang.shape, 1) & (HEAD_DIM - 1)) < HEAD_DIM // 2
    cos_ref[...] = c
    sa_ref[...] = jnp.where(first_half, -s, 0.0)
    sb_ref[...] = jnp.where(first_half, 0.0, s)


def _inproj_kernel(x_ref, g_ref, w_ref, cos_ref, sa_ref, sb_ref, lb_ref, zt_ref, lf_ref):
    h = _rms(x_ref[...], g_ref[...]).astype(BF16)
    cos = cos_ref[...]
    sa = sa_ref[...]
    sb = sb_ref[...]

    def proj(c0, width):
        return _dot(h, w_ref[:, c0:c0 + width])

    def rope(z):
        return z * cos + pltpu.roll(z, 96, axis=1) * sa + pltpu.roll(z, 32, axis=1) * sb

    def put(c0, val):
        zt_ref[:, c0:c0 + val.shape[1]] = val.astype(BF16)

    def put_rope(c0, z, scale):
        for j in range(z.shape[1] // 128):
            put(c0 + 128 * j, rope(z[:, 128 * j:128 * (j + 1)]) * scale)

    put(AQ, proj(AQ, A_WIDTH))
    fl = proj(AF, A_WIDTH)
    e = jnp.exp(-jnp.abs(fl))
    inv = 1.0 / (1.0 + e)
    sig_pos = jnp.where(fl >= 0, inv, e * inv)
    sig_neg = jnp.where(fl >= 0, e * inv, inv)
    lb = lb_ref[...]
    lf_ref[...] = jnp.log(lb + (1.0 - lb) * sig_pos) * LOG2E
    put(AF, (1.0 - lb) * sig_neg)
    put(AI, proj(AI, A_WIDTH))
    g = proj(AG, A_WIDTH)
    put(AG, g * _sigmoid(g))
    put_rope(BQ, proj(BQ, B_WIDTH), HEAD_DIM ** -0.5 * LOG2E)
    kv = proj(BK, 2 * B_KV_WIDTH)
    put_rope(BK, kv[:, :B_KV_WIDTH], 1.0)
    put(BV, kv[:, B_KV_WIDTH:])
    put_rope(CQ, proj(CQ, C_WIDTH), 1.0)
    put_rope(CK, proj(CK, C_WIDTH), HEAD_DIM ** -0.5)
    put(CV, proj(CV, C_WIDTH))
    g = proj(CG, C_WIDTH)
    put(CG, g * _sigmoid(g))


def _inproj(r, gain, w_in, cos, sa, sb, lb_all, layer):
    t = r.shape[0]
    tm = INPROJ_TILE
    row = lambda i: (i, 0)
    return pl.pallas_call(
        _inproj_kernel,
        grid=(t // tm,),
        in_specs=[
            pl.BlockSpec((tm, D_MODEL), row),
            pl.BlockSpec((None, 1, D_MODEL), lambda i: (layer, 0, 0)),
            pl.BlockSpec((None, D_MODEL, D_IN), lambda i: (layer, 0, 0)),
            pl.BlockSpec((tm, 128), row),
            pl.BlockSpec((tm, 128), row),
            pl.BlockSpec((tm, 128), row),
            pl.BlockSpec((None, 1, A_WIDTH), lambda i: (layer, 0, 0)),
        ],
        out_specs=[pl.BlockSpec((tm, D_IN), row), pl.BlockSpec((tm, A_WIDTH), row)],
        out_shape=[jax.ShapeDtypeStruct((t, D_IN), BF16), jax.ShapeDtypeStruct((t, A_WIDTH), F32)],
        compiler_params=_params("parallel"),
        name="inproj",
    )(r, gain, w_in, cos, sa, sb, lb_all)


def _hgrn_stages(q_ref, k_ref, v_ref, sg_ref, lf_ref, gn_ref, o_ref, st_ref, m3_ref, lev_ref, nchunk):
    width = A_WIDTH
    cl = HGRN_CHUNK
    nlev = cl.bit_length() - 1
    nhead = width // HEAD_DIM

    def init():
        st_ref[...] = jnp.zeros_like(st_ref)
        t = _iota((cl, 3 * cl), 0)
        s = _iota((cl, 3 * cl), 1) & (cl - 1)
        m3_ref[0:cl, :] = (s <= t).astype(BF16)
        m3_ref[cl:2 * cl, :] = (s <= (t & -4) + 1).astype(BF16)
        tq = _iota((cl, width), 0)
        sk = _iota((cl, width), 1) & (cl - 1)
        diff = tq ^ sk
        lvl = jnp.zeros((cl, width), jnp.int32)
        for j in range(1, nlev):
            lvl = lvl + (diff >= (1 << j)).astype(jnp.int32)
        lev_ref[...] = jnp.where(tq > sk, lvl, jnp.where(tq == sk, -2, -1))

    row = _iota((cl, width), 0)
    head = _iota((cl, width), 1) >> 6
    hmask = [(head == h).astype(BF16) for h in range(nhead)]
    same_head = (_iota((width, width), 0) >> 6) == (_iota((width, width), 1) >> 6)
    sel = same_head.astype(BF16)
    gain = gn_ref[...]
    chunks = [None] * nchunk

    def by_head(x):
        return jnp.concatenate([x * hmask[h] for h in range(nhead)], axis=0)

    def by_head_t(x):
        return jnp.concatenate([x] * nhead, axis=0).T * sel

    def sums_stage(c):
        sl = slice(c * cl, (c + 1) * cl)
        lf = lf_ref[sl, :]
        hi = lf.astype(BF16)
        rem = lf - hi.astype(F32)
        mid = rem.astype(BF16)
        lo = (rem - mid.astype(F32)).astype(BF16)
        chunks[c] = dict(sl=sl, lf=lf, sums=_dot(m3_ref[...], jnp.concatenate([hi, mid, lo], axis=0)))

    def level_stage(c):
        ch = chunks[c]
        sl, lf, sums = ch["sl"], ch["lf"], ch["sums"]
        qf = q_ref[sl, :].astype(F32)
        kf = k_ref[sl, :].astype(F32)
        bc = sums[0:cl]
        bl = bc[cl - 1:cl, :]
        scores = []
        for l in range(nlev):
            m = 1 << l
            second = (row & m) != 0
            if l == 0:
                e = jnp.where(second, jnp.exp2(lf), 1.0)
            else:
                if l == 1:
                    ref = sums[cl:2 * cl]
                else:
                    ref = jnp.concatenate(
                        [jnp.broadcast_to(bc[g + m - 1:g + m, :], (2 * m, width)) for g in range(0, cl, 2 * m)], axis=0)
                e = jnp.exp2(-jnp.abs(bc - ref))
            x = (jnp.where(second, qf, kf) * e).astype(BF16)
            scores.append(_dot(x, by_head_t(x)))
        diag = _dot((qf * kf).astype(BF16), sel)
        lev = lev_ref[...]
        sc = jnp.zeros((cl, width), F32)
        for l in range(nlev):
            sc = jnp.where(lev == l, scores[l], sc)
        sc = jnp.where(lev == -2, diag, sc)
        ch.update(sc=sc.astype(BF16),
                  qe=(qf * jnp.exp2(bc)).astype(BF16), kd=(kf * jnp.exp2(bl - bc)).astype(BF16), ebl=jnp.exp2(bl))

    def state_stage(c):
        ch = chunks[c]
        vb = v_ref[ch["sl"], :]
        st = st_ref[...]
        ch["o"] = _dot_nt(ch["qe"], st.astype(BF16))
        st_ref[...] = st * ch["ebl"] + jnp.where(same_head, _dot_tn(vb, ch["kd"]), 0.0)

    def value_stage(c):
        ch = chunks[c]
        ch["o"] = ch["o"] + _dot(ch.pop("sc"), by_head(v_ref[ch["sl"], :]))

    def norm_stage(c):
        ch = chunks[c]
        o = ch["o"]
        ms = _dot((o * o).astype(BF16), sel) * (1.0 / HEAD_DIM)
        y = o * lax.rsqrt(ms + EPS) * gain * sg_ref[ch["sl"], :].astype(F32)
        o_ref[ch["sl"], :] = y.astype(BF16)
        chunks[c] = None

    stages = [functools.partial(sums_stage, 0)]
    for c in range(nchunk):
        if c + 1 < nchunk:
            stages.append(functools.partial(sums_stage, c + 1))
        stages.append(functools.partial(level_stage, c))
        stages.append(functools.partial(state_stage, c))
        if c > 0:
            stages.append(functools.partial(norm_stage, c - 1))
        stages.append(functools.partial(value_stage, c))
    stages.append(functools.partial(norm_stage, nchunk - 1))
    return init, stages


def _ret_stages(lg_ref, q_ref, k_ref, v_ref, sg_ref, o_ref, s_ref, d_ref, qd_ref, kd_ref, cd_ref, nchunk):
    width = C_WIDTH
    cl = RET_CHUNK
    nhead = width // HEAD_DIM
    head_row = _iota((1, width), 1) >> 6

    def init():
        s_ref[...] = jnp.zeros_like(s_ref)
        lgl = jnp.zeros((1, width), F32)
        rel = _iota((cl, cl), 0) - _iota((cl, cl), 1)
        causal = rel >= 0
        relf = jnp.where(causal, rel, 0).astype(F32)
        for h in range(nhead):
            lgl = jnp.where(head_row == h, lg_ref[h], lgl)
            d_ref[:, cl * h:cl * (h + 1)] = jnp.where(causal, jnp.exp(relf * lg_ref[h]), 0.0)
        t = _iota((cl, width), 0).astype(F32)
        qd_ref[...] = jnp.exp((t + 1.0) * lgl)
        kd_ref[...] = jnp.exp((cl - 1.0 - t) * lgl)
        cd_ref[...] = jnp.exp(float(cl) * lgl)

    head = _iota((cl, width), 1) >> 6
    hmask = [(head == h).astype(BF16) for h in range(nhead)]
    same_head = (_iota((width, width), 0) >> 6) == (_iota((width, width), 1) >> 6)
    sel = same_head.astype(BF16)
    sel_t = ((_iota((width, nhead * cl), 0) >> 6)
             == (_iota((width, nhead * cl), 1) >> (cl.bit_length() - 1))).astype(BF16)
    chunks = [None] * nchunk

    def by_head(x):
        return jnp.concatenate([x * hmask[h] for h in range(nhead)], axis=0)

    def score_stage(c):
        sl = slice(c * cl, (c + 1) * cl)
        k4t = jnp.concatenate([k_ref[sl, :]] * nhead, axis=0).T
        chunks[c] = dict(sl=sl, scores=(_dot(q_ref[sl, :], k4t * sel_t) * d_ref[...]).astype(BF16))

    def state_stage(c):
        ch = chunks[c]
        sl = ch["sl"]
        st = s_ref[...]
        ch["o"] = _dot((q_ref[sl, :].astype(F32) * qd_ref[...]).astype(BF16), st.astype(BF16))
        kdec = (k_ref[sl, :].astype(F32) * kd_ref[...]).astype(BF16)
        s_ref[...] = st * cd_ref[...] + jnp.where(same_head, _dot_tn(kdec, v_ref[sl, :]), 0.0)

    def value_stage(c):
        ch = chunks[c]
        ch["o"] = ch["o"] + _dot(ch.pop("scores"), by_head(v_ref[ch["sl"], :]))

    def norm_stage(c):
        ch = chunks[c]
        o = ch["o"]
        ms = _dot((o * o).astype(BF16), sel) * (1.0 / HEAD_DIM)
        y = o * lax.rsqrt(ms + EPS) * sg_ref[ch["sl"], :].astype(F32)
        o_ref[ch["sl"], :] = y.astype(BF16)
        chunks[c] = None

    stages = [functools.partial(score_stage, 0)]
    for c in range(nchunk):
        if c + 1 < nchunk:
            stages.append(functools.partial(score_stage, c + 1))
        stages.append(functools.partial(state_stage, c))
        if c > 0:
            stages.append(functools.partial(norm_stage, c - 1))
        stages.append(functools.partial(value_stage, c))
    stages.append(functools.partial(norm_stage, nchunk - 1))
    return init, stages


def _swa_stages(sink_ref, q_ref, k_ref, v_ref, kp_ref, vp_ref, o_ref, seq_first, nblk):
    w = WINDOW
    not_first = jnp.logical_not(seq_first)
    low = _iota((2 * w, 128), 1) < HEAD_DIM
    upper = _iota((w, w), 1) > _iota((w, w), 0)
    upper_first = jnp.logical_and(upper, not_first)
    up_b = upper.astype(BF16)
    lo_b = jnp.logical_not(upper).astype(BF16)
    ones = jnp.ones((2 * w, 128), BF16)
    zero = jnp.zeros((2 * w, 128), BF16)
    second_head = _iota((2 * w, 1), 0) >= w
    upper2 = jnp.concatenate([upper, upper], axis=0)
    upper_first2 = jnp.concatenate([upper_first, upper_first], axis=0)
    up_b2 = jnp.concatenate([up_b, up_b], axis=0)
    lo_b2 = jnp.concatenate([lo_b, lo_b], axis=0)

    def operands(blk):
        cur = slice(blk * w, (blk + 1) * w)
        if blk == 0:
            kprev, vprev = kp_ref[...], vp_ref[...]
        else:
            prev = slice((blk - 1) * w, blk * w)
            kprev, vprev = k_ref[prev, :], v_ref[prev, :]
        kk = jnp.concatenate([kprev, k_ref[cur, :]], axis=0)
        vv = jnp.concatenate([vprev, v_ref[cur, :]], axis=0)
        kr = pltpu.roll(kk.astype(F32), HEAD_DIM, axis=1).astype(BF16)
        vr = pltpu.roll(vv.astype(F32), HEAD_DIM, axis=1).astype(BF16)
        ops = dict(cur=cur)
        for g in range(2):
            for half in range(2):
                keep = low if half == 0 else jnp.logical_not(low)
                ops["kt", g, half] = jnp.where(keep, kk if g == half else kr, zero).T
                ops["v", g, half] = jnp.concatenate([jnp.where(keep, vv if g == half else vr, zero), ones], axis=1)
        return ops

    items = [(blk, g, half) for blk in range(nblk) for g in range(2) for half in range(2)]
    ops_of = {}
    pending = {}
    state = {}

    def score_stage(n):
        blk, g, half = items[n]
        if blk not in ops_of:
            ops_of[blk] = operands(blk)
        cur = ops_of[blk]["cur"]
        q2 = jnp.concatenate([q_ref[cur, 256 * g:256 * g + 128], q_ref[cur, 256 * g + 128:256 * g + 256]], axis=0)
        both = _dot(q2, ops_of[blk]["kt", g, half])
        s_prev, s_cur = both[:, :w], both[:, w:]
        sink = jnp.where(second_head, sink_ref[4 * g + 2 + half], sink_ref[4 * g + half]) * LOG2E
        if blk == 0:
            s = jnp.where(upper_first2, s_prev, jnp.where(upper2, NEG_BIG, s_cur))
        else:
            s = jnp.where(upper2, s_prev, s_cur)
        m = jnp.maximum(jnp.max(s, axis=-1, keepdims=True), sink)
        p = jnp.exp2(s - m).astype(BF16)
        pending[n] = (jnp.concatenate([p * up_b2, p * lo_b2], axis=1), jnp.exp2(sink - m))

    def value_stage(n):
        blk, g, half = items[n]
        cur_ops = ops_of[blk]
        probs, sink_term = pending.pop(n)
        ol = _dot(probs, cur_ops["v", g, half])
        part = ol[:, :128] / (ol[:, 128:] + sink_term)
        if half == 0:
            state["acc"] = part
        else:
            both = (state.pop("acc") + part).astype(BF16)
            o_ref[cur_ops["cur"], 256 * g:256 * g + 128] = both[:w]
            o_ref[cur_ops["cur"], 256 * g + 128:256 * g + 256] = both[w:]

    stages = []
    for n in range(len(items) + SWA_LOOKAHEAD):
        if n < len(items):
            stages.append(functools.partial(score_stage, n))
        if n >= SWA_LOOKAHEAD:
            stages.append(functools.partial(value_stage, n - SWA_LOOKAHEAD))
    return stages


def _tail_stages(r_ref, mix_ref, p_ref, wo_ref, gf_ref, wg_ref, wu_ref, cw_ref, cb_ref, wd_ref,
                 gp_ref, wpg_ref, wpp_ref, gfin_ref, o_ref, lhs_ref, ge_ref, act_ref, halo_ref, seq_first, final):
    tm = r_ref.shape[0]
    val = {}

    def head_stage():
        r1 = r_ref[...] + _dot(mix_ref[...], wo_ref[...])
        halo = halo_ref[...]
        lhs_ref[0:CONV_HALO, :] = jnp.where(seq_first, jnp.zeros_like(halo), halo)
        h2 = _rms(r1, gf_ref[...]).astype(BF16)
        lhs_ref[CONV_HALO:, :] = h2
        halo_ref[...] = h2[tm - CONV_HALO:, :]
        val["r1"] = r1

    def mlp_stage(j):
        cols = slice(j * FF_SUB, (j + 1) * FF_SUB)
        ge = ge_ref.at[j % 2]
        ge[...] = _dot(lhs_ref[...], wg_ref[:, cols])
        cw = cw_ref[:, cols]
        gate = (ge[pl.ds(CONV_HALO - 2, tm), :] * cw[0:1, :]
                + ge[pl.ds(CONV_HALO - 1, tm), :] * cw[1:2, :]
                + ge[pl.ds(CONV_HALO, tm), :] * cw[2:3, :]
                + cb_ref[:, cols])
        up = _dot(lhs_ref[CONV_HALO:, :], wu_ref[:, cols])
        act_ref[:, cols] = (jax.nn.gelu(gate, approximate=True) * up).astype(BF16)

    def down_stage(n):
        cols = slice(n * FF_SUB, (n + 1) * FF_SUB)
        val["down", n] = _dot(act_ref[...], wd_ref[:, cols])

    def end_stage():
        r2 = val.pop("r1") + jnp.concatenate([val.pop(("down", n)) for n in range(D_MODEL // FF_SUB)], axis=1)
        gate = _sigmoid(_dot(_rms(r2, gp_ref[...]).astype(BF16), wpg_ref[...]))
        r3 = r2 + _dot(p_ref[...].astype(BF16), wpp_ref[...]) * gate
        o_ref[...] = _rms(r3, gfin_ref[...]) if final else r3

    stages = [head_stage]
    stages += [functools.partial(mlp_stage, j) for j in range(D_FF // FF_SUB)]
    stages += [functools.partial(down_stage, n) for n in range(D_MODEL // FF_SUB)]
    stages.append(end_stage)
    return stages


def _deal(parts):
    keyed = []
    for k, part in enumerate(parts):
        for n, stage in enumerate(part):
            keyed.append(((n + 0.5) / len(part), k, n, stage))
    keyed.sort(key=lambda e: e[:3])
    return [e[3] for e in keyed]


def _body_kernel(sink_ref, lg_ref,
                 hq_ref, hk_ref, hv_ref, hsg_ref, lf_ref, gn_ref,
                 sq_ref, sk_ref, sv_ref, skp_ref, svp_ref,
                 rq_ref, rk_ref, rv_ref, rsg_ref,
                 r_ref, p_ref, wo_ref, gf_ref, wg_ref, wu_ref, cw_ref, cb_ref, wd_ref, gp_ref, wpg_ref, wpp_ref,
                 gfin_ref, o_ref,
                 mixnew_ref, mixold_ref, lhs_ref, ge_ref, act_ref, halo_ref,
                 hst_ref, m3_ref, lev_ref, rst_ref, d_ref, qd_ref, kd_ref, cd_ref,
                 *, ntiles, tiles_per_seq, final):
    tm = r_ref.shape[0]
    i = pl.program_id(0)
    mix_first = jnp.minimum(i, ntiles - 1) % tiles_per_seq == 0
    tail_first = jnp.maximum(i - 1, 0) % tiles_per_seq == 0

    def emit(with_mixers, with_tail):
        parts = []
        if with_tail:
            parts.append(_tail_stages(r_ref, mixold_ref, p_ref, wo_ref, gf_ref, wg_ref, wu_ref, cw_ref, cb_ref,
                                      wd_ref, gp_ref, wpg_ref, wpp_ref, gfin_ref, o_ref, lhs_ref, ge_ref, act_ref,
                                      halo_ref, tail_first, final))
        if with_mixers:
            hgrn_init, hgrn = _hgrn_stages(hq_ref, hk_ref, hv_ref, hsg_ref, lf_ref, gn_ref,
                                           mixnew_ref.at[:, 0:A_WIDTH], hst_ref, m3_ref, lev_ref, tm // HGRN_CHUNK)
            ret_init, ret = _ret_stages(lg_ref, rq_ref, rk_ref, rv_ref, rsg_ref,
                                        mixnew_ref.at[:, A_WIDTH + B_WIDTH:], rst_ref, d_ref, qd_ref, kd_ref,
                                        cd_ref, tm // RET_CHUNK)
            swa = _swa_stages(sink_ref, sq_ref, sk_ref, sv_ref, skp_ref, svp_ref,
                              mixnew_ref.at[:, A_WIDTH:A_WIDTH + B_WIDTH], mix_first, tm // WINDOW)
            parts += [hgrn, swa, ret]

            @pl.when(mix_first)
            def _():
                hgrn_init()
                ret_init()

        for stage in _deal(parts):
            stage()
        if with_mixers:
            mixold_ref[...] = mixnew_ref[...]

    @pl.when(i == 0)
    def _():
        halo_ref[...] = jnp.zeros_like(halo_ref)
        emit(True, False)

    @pl.when(jnp.logical_and(i > 0, i < ntiles))
    def _():
        emit(True, True)

    @pl.when(i == ntiles)
    def _():
        emit(False, True)


def _layer_body(r, zt, lf, p, sinks, log_g, gnorm, w_out, ffn_gain, w_gate, w_up, conv_w, conv_b, w_down,
                ple_gain, w_pg, w_pp, final_gain, layer, seq_len, final):
    t = r.shape[0]
    tm = TOKEN_TILE
    ntiles = t // tm
    per = tm // WINDOW
    once = pl.Buffered(1)
    smem = pl.BlockSpec(memory_space=pltpu.SMEM)

    def mix_tile(i):
        return jnp.minimum(i, ntiles - 1)

    def tail_tile(i):
        return jnp.maximum(i - 1, 0)

    def zcol(width, c0):
        return pl.BlockSpec((tm, width), lambda i: (mix_tile(i), c0 // width))

    def zprev(c0):
        return pl.BlockSpec((WINDOW, B_KV_WIDTH),
                            lambda i: (jnp.maximum(mix_tile(i) * per - 1, 0), c0 // B_KV_WIDTH))

    def whole(shape):
        return pl.BlockSpec((None,) + shape, lambda i: (layer,) + (0,) * len(shape), pipeline_mode=once)

    return pl.pallas_call(
        functools.partial(_body_kernel, ntiles=ntiles, tiles_per_seq=seq_len // tm, final=final),
        grid=(ntiles + 1,),
        in_specs=[smem, smem,
                  zcol(A_WIDTH, AQ), zcol(A_WIDTH, AF), zcol(A_WIDTH, AI), zcol(A_WIDTH, AG),
                  pl.BlockSpec((tm, A_WIDTH), lambda i: (mix_tile(i), 0)),
                  whole((1, A_WIDTH)),
                  zcol(B_WIDTH, BQ), zcol(B_KV_WIDTH, BK), zcol(B_KV_WIDTH, BV), zprev(BK), zprev(BV),
                  zcol(C_WIDTH, CQ), zcol(C_WIDTH, CK), zcol(C_WIDTH, CV), zcol(C_WIDTH, CG),
                  pl.BlockSpec((tm, D_MODEL), lambda i: (tail_tile(i), 0)),
                  pl.BlockSpec((None, tm, PLE_DIM), lambda i: (layer, tail_tile(i), 0)),
                  whole((D_MODEL, D_MODEL)),
                  whole((1, D_MODEL)),
                  whole((D_MODEL, D_FF)),
                  whole((D_MODEL, D_FF)),
                  whole((3, D_FF)),
                  whole((1, D_FF)),
                  whole((D_FF, D_MODEL)),
                  whole((1, D_MODEL)),
                  whole((D_MODEL, D_MODEL)),
                  whole((PLE_DIM, D_MODEL)),
                  pl.BlockSpec((1, D_MODEL), lambda i: (0, 0), pipeline_mode=once)],
        out_specs=pl.BlockSpec((tm, D_MODEL), lambda i: (tail_tile(i), 0)),
        out_shape=jax.ShapeDtypeStruct((t, D_MODEL), F32),
        scratch_shapes=[pltpu.VMEM((tm, D_MODEL), BF16),
                        pltpu.VMEM((tm, D_MODEL), BF16),
                        pltpu.VMEM((tm + CONV_HALO, D_MODEL), BF16),
                        pltpu.VMEM((2, tm + CONV_HALO, FF_SUB), F32),
                        pltpu.VMEM((tm, D_FF), BF16),
                        pltpu.VMEM((CONV_HALO, D_MODEL), BF16),
                        pltpu.VMEM((A_WIDTH, A_WIDTH), F32),
                        pltpu.VMEM((2 * HGRN_CHUNK, 3 * HGRN_CHUNK), BF16),
                        pltpu.VMEM((HGRN_CHUNK, A_WIDTH), jnp.int32),
                        pltpu.VMEM((C_WIDTH, C_WIDTH), F32),
                        pltpu.VMEM((RET_CHUNK, C_WIDTH // HEAD_DIM * RET_CHUNK), F32),
                        pltpu.VMEM((RET_CHUNK, C_WIDTH), F32),
                        pltpu.VMEM((RET_CHUNK, C_WIDTH), F32),
                        pltpu.VMEM((1, C_WIDTH), F32)],
        compiler_params=_params("arbitrary"),
        name="layer_body",
    )(sinks, log_g,
      zt, zt, zt, zt, lf, gnorm,
      zt, zt, zt, zt, zt,
      zt, zt, zt, zt,
      r, p, w_out, ffn_gain, w_gate, w_up, conv_w, conv_b, w_down, ple_gain, w_pg, w_pp, final_gain)


def kernel(x, p, positions, attn_norm, w_in, hgrn_lb, hgrn_gnorm, attn_sinks, w_out, ffn_norm, w_gate, w_up,
           conv_w, conv_b, w_down, ple_norm, w_ple_gate, w_ple_proj, final_norm):
    b, s, d = x.shape
    depth = w_in.shape[0]
    t = b * s
    assert d == D_MODEL and s % TOKEN_TILE == 0 and t % INPROJ_TILE == 0
    assert TOKEN_TILE % RET_CHUNK == 0 and TOKEN_TILE % HGRN_CHUNK == 0 and TOKEN_TILE % WINDOW == 0

    vec = lambda a: a.reshape(depth, 1, a.shape[-1])
    bf = lambda a: a.astype(BF16)

    lb_all = pl.pallas_call(
        _lower_bound_kernel, out_shape=jax.ShapeDtypeStruct(hgrn_lb.shape, F32), name="hgrn_lower_bound",
    )(hgrn_lb.astype(F32))

    inv = 1.0 / (ROPE_THETA ** (jnp.arange(0, HEAD_DIM, 2, dtype=F32) / HEAD_DIM))
    inv = jnp.tile(inv, 4).reshape(1, 128)
    tt = INPROJ_TILE
    cos, sin_a, sin_b = pl.pallas_call(
        _rope_table_kernel,
        grid=(t // tt,),
        in_specs=[pl.BlockSpec((tt, 1), lambda i: (i, 0)), pl.BlockSpec((1, 128), lambda i: (0, 0))],
        out_specs=[pl.BlockSpec((tt, 128), lambda i: (i, 0))] * 3,
        out_shape=[jax.ShapeDtypeStruct((t, 128), F32)] * 3,
        compiler_params=_params("parallel"),
        name="rope_tables",
    )(positions.reshape(t, 1), inv)

    nhead_c = C_WIDTH // HEAD_DIM
    log_g = jnp.log(1.0 - 2.0 ** (-5.0 - jnp.arange(nhead_c, dtype=F32)))

    w_in_b, w_out_b, w_gate_b, w_up_b, w_down_b = bf(w_in), bf(w_out), bf(w_gate), bf(w_up), bf(w_down)
    w_pg_b, w_pp_b = bf(w_ple_gate), bf(w_ple_proj)
    attn_norm_v, ffn_norm_v, ple_norm_v = vec(attn_norm), vec(ffn_norm), vec(ple_norm)
    gnorm_v, conv_b_v, lb_v = vec(hgrn_gnorm), vec(conv_b), vec(lb_all)
    p_flat = p.reshape(depth, t, PLE_DIM)

    r = x.reshape(t, d).astype(F32)
    for i in range(depth):
        zt, lf = _inproj(r, attn_norm_v, w_in_b, cos, sin_a, sin_b, lb_v, i)
        r = _layer_body(r, zt, lf, p_flat, attn_sinks[i].astype(F32), log_g, gnorm_v, w_out_b, ffn_norm_v,
                        w_gate_b, w_up_b, conv_w, conv_b_v, w_down_b, ple_norm_v, w_pg_b, w_pp_b,
                        final_norm.reshape(1, d), i, s, i == depth - 1)
    return r.reshape(b, s, d).astype(x.dtype)
```

```python
import functools

import jax
import jax.numpy as jnp
from jax import lax
from jax.experimental import pallas as pl
from jax.experimental.pallas import tpu as pltpu

F32 = jnp.float32
BF16 = jnp.bfloat16

D_MODEL = 1024
HEAD_DIM = 64
A_WIDTH = 256
B_WIDTH = 512
B_KV_WIDTH = 128
C_WIDTH = 256
WINDOW = 128
D_IN = 2816
D_FF = 2816
PLE_DIM = 256
ROPE_THETA = 10000.0
EPS = 1e-6
NEG_BIG = -1e30
LOG2E = 1.4426950408889634

AQ, AF, AI, AG = 0, 256, 512, 768
BQ, BK, BV = 1024, 1536, 1664
CQ, CK, CV, CG = 1792, 2048, 2304, 2560

TOKEN_TILE = 512
INPROJ_TILE = 1024
SWA_LOOKAHEAD = 2
HGRN_CHUNK = 64
RET_CHUNK = 128
FF_SUB = 256
CONV_HALO = 16
VMEM_LIMIT = 56 * 1024 * 1024


def _dot(a, b):
    return jnp.dot(a, b, preferred_element_type=F32)


def _dot_nt(a, b):
    return lax.dot_general(a, b, (((1,), (1,)), ((), ())), preferred_element_type=F32)


def _dot_tn(a, b):
    return lax.dot_general(a, b, (((0,), (0,)), ((), ())), preferred_element_type=F32)


def _iota(shape, dim):
    return lax.broadcasted_iota(jnp.int32, shape, dim)


def _sigmoid(x):
    return 1.0 / (1.0 + jnp.exp(-x))


def _rms(x, gain):
    y = x * lax.rsqrt(jnp.mean(x * x, axis=-1, keepdims=True) + EPS)
    return y * gain


def _params(*sem):
    return pltpu.CompilerParams(dimension_semantics=sem, vmem_limit_bytes=VMEM_LIMIT)


def _lower_bound_kernel(lb_ref, o_ref):
    x = lb_ref[...]
    depth = x.shape[0]
    e = jnp.exp(x - jnp.max(x, axis=0, keepdims=True))
    p = e / jnp.sum(e, axis=0, keepdims=True)
    row = _iota(x.shape, 0)
    acc = jnp.zeros_like(x)
    for j in range(depth):
        acc = acc + jnp.where(row >= j, p[j:j + 1, :], 0.0)
    o_ref[...] = acc - p[0:1, :]


def _rope_table_kernel(pos_ref, inv_ref, cos_ref, sa_ref, sb_ref):
    ang = pos_ref[...].astype(F32) * inv_ref[...]
    c = jnp.cos(ang)
    s = jnp.sin(ang)
    first_half = (_iota(ang.shape, 1) & (HEAD_DIM - 1)) < HEAD_DIM // 2
    cos_ref[...] = c
    sa_ref[...] = jnp.where(first_half, -s, 0.0)
    sb_ref[...] = jnp.where(first_half, 0.0, s)


def _inproj_kernel(x_ref, g_ref, w_ref, cos_ref, sa_ref, sb_ref, lb_ref, zt_ref, lf_ref):
    h = _rms(x_ref[...], g_ref[...]).astype(BF16)
    cos = cos_ref[...]
    sa = sa_ref[...]
    sb = sb_ref[...]

    def proj(c0, width):
        return _dot(h, w_ref[:, c0:c0 + width])

    def rope(z):
        return z * cos + pltpu.roll(z, 96, axis=1) * sa + pltpu.roll(z, 32, axis=1) * sb

    def put(c0, val):
        zt_ref[:, c0:c0 + val.shape[1]] = val.astype(BF16)

    def put_rope(c0, z, scale):
        for j in range(z.shape[1] // 128):
            put(c0 + 128 * j, rope(z[:, 128 * j:128 * (j + 1)]) * scale)

    put(AQ, proj(AQ, A_WIDTH))
    fl = proj(AF, A_WIDTH)
    e = jnp.exp(-jnp.abs(fl))
    inv = 1.0 / (1.0 + e)
    sig_pos = jnp.where(fl >= 0, inv, e * inv)
    sig_neg = jnp.where(fl >= 0, e * inv, inv)
    lb = lb_ref[...]
    lf_ref[...] = jnp.log(lb + (1.0 - lb) * sig_pos) * LOG2E
    put(AF, (1.0 - lb) * sig_neg)
    put(AI, proj(AI, A_WIDTH))
    g = proj(AG, A_WIDTH)
    put(AG, g * _sigmoid(g))
    put_rope(BQ, proj(BQ, B_WIDTH), HEAD_DIM ** -0.5 * LOG2E)
    kv = proj(BK, 2 * B_KV_WIDTH)
    put_rope(BK, kv[:, :B_KV_WIDTH], 1.0)
    put(BV, kv[:, B_KV_WIDTH:])
    put_rope(CQ, proj(CQ, C_WIDTH), 1.0)
    put_rope(CK, proj(CK, C_WIDTH), HEAD_DIM ** -0.5)
    put(CV, proj(CV, C_WIDTH))
    g = proj(CG, C_WIDTH)
    put(CG, g * _sigmoid(g))


def _inproj(r, gain, w_in, cos, sa, sb, lb_all, layer):
    t = r.shape[0]
    tm = INPROJ_TILE
    row = lambda i: (i, 0)
    return pl.pallas_call(
        _inproj_kernel,
        grid=(t // tm,),
        in_specs=[
            pl.BlockSpec((tm, D_MODEL), row),
            pl.BlockSpec((None, 1, D_MODEL), lambda i: (layer, 0, 0)),
            pl.BlockSpec((None, D_MODEL, D_IN), lambda i: (layer, 0, 0)),
            pl.BlockSpec((tm, 128), row),
            pl.BlockSpec((tm, 128), row),
            pl.BlockSpec((tm, 128), row),
            pl.BlockSpec((None, 1, A_WIDTH), lambda i: (layer, 0, 0)),
        ],
        out_specs=[pl.BlockSpec((tm, D_IN), row), pl.BlockSpec((tm, A_WIDTH), row)],
        out_shape=[jax.ShapeDtypeStruct((t, D_IN), BF16), jax.ShapeDtypeStruct((t, A_WIDTH), F32)],
        compiler_params=_params("parallel"),
        name="inproj",
    )(r, gain, w_in, cos, sa, sb, lb_all)


def _hgrn_stages(q_ref, k_ref, v_ref, sg_ref, lf_ref, gn_ref, o_ref, st_ref, m3_ref, lev_ref, nchunk):
    width = A_WIDTH
    cl = HGRN_CHUNK
    nlev = cl.bit_length() - 1
    nhead = width // HEAD_DIM

    def init():
        st_ref[...] = jnp.zeros_like(st_ref)
        t = _iota((cl, 3 * cl), 0)
        s = _iota((cl, 3 * cl), 1) & (cl - 1)
        m3_ref[0:cl, :] = (s <= t).astype(BF16)
        m3_ref[cl:2 * cl, :] = (s <= (t & -4) + 1).astype(BF16)
        tq = _iota((cl, width), 0)
        sk = _iota((cl, width), 1) & (cl - 1)
        diff = tq ^ sk
        lvl = jnp.zeros((cl, width), jnp.int32)
        for j in range(1, nlev):
            lvl = lvl + (diff >= (1 << j)).astype(jnp.int32)
        lev_ref[...] = jnp.where(tq > sk, lvl, jnp.where(tq == sk, -2, -1))

    row = _iota((cl, width), 0)
    head = _iota((cl, width), 1) >> 6
    hmask = [(head == h).astype(BF16) for h in range(nhead)]
    same_head = (_iota((width, width), 0) >> 6) == (_iota((width, width), 1) >> 6)
    sel = same_head.astype(BF16)
    gain = gn_ref[...]
    chunks = [None] * nchunk

    def by_head(x):
        return jnp.concatenate([x * hmask[h] for h in range(nhead)], axis=0)

    def by_head_t(x):
        return jnp.concatenate([x] * nhead, axis=0).T * sel

    def sums_stage(c):
        sl = slice(c * cl, (c + 1) * cl)
        lf = lf_ref[sl, :]
        hi = lf.astype(BF16)
        rem = lf - hi.astype(F32)
        mid = rem.astype(BF16)
        lo = (rem - mid.astype(F32)).astype(BF16)
        chunks[c] = dict(sl=sl, lf=lf, sums=_dot(m3_ref[...], jnp.concatenate([hi, mid, lo], axis=0)))

    def level_stage(c):
        ch = chunks[c]
        sl, lf, sums = ch["sl"], ch["lf"], ch["sums"]
        qf = q_ref[sl, :].astype(F32)
        kf = k_ref[sl, :].astype(F32)
        bc = sums[0:cl]
        bl = bc[cl - 1:cl, :]
        scores = []
        for l in range(nlev):
            m = 1 << l
            second = (row & m) != 0
            if l == 0:
                e = jnp.where(second, jnp.exp2(lf), 1.0)
            else:
                if l == 1:
                    ref = sums[cl:2 * cl]
                else:
                    ref = jnp.concatenate(
                        [jnp.broadcast_to(bc[g + m - 1:g + m, :], (2 * m, width)) for g in range(0, cl, 2 * m)], axis=0)
                e = jnp.exp2(-jnp.abs(bc - ref))
            x = (jnp.where(second, qf, kf) * e).astype(BF16)
            scores.append(_dot(x, by_head_t(x)))
        diag = _dot((qf * kf).astype(BF16), sel)
        lev = lev_ref[...]
        sc = jnp.zeros((cl, width), F32)
        for l in range(nlev):
            sc = jnp.where(lev == l, scores[l], sc)
        sc = jnp.where(lev == -2, diag, sc)
        ch.update(sc=sc.astype(BF16),
                  qe=(qf * jnp.exp2(bc)).astype(BF16), kd=(kf * jnp.exp2(bl - bc)).astype(BF16), ebl=jnp.exp2(bl))

    def state_stage(c):
        ch = chunks[c]
        vb = v_ref[ch["sl"], :]
        st = st_ref[...]
        ch["o"] = _dot_nt(ch["qe"], st.astype(BF16))
        st_ref[...] = st * ch["ebl"] + jnp.where(same_head, _dot_tn(vb, ch["kd"]), 0.0)

    def value_stage(c):
        ch = chunks[c]
        ch["o"] = ch["o"] + _dot(ch.pop("sc"), by_head(v_ref[ch["sl"], :]))

    def norm_stage(c):
        ch = chunks[c]
        o = ch["o"]
        ms = _dot((o * o).astype(BF16), sel) * (1.0 / HEAD_DIM)
        y = o * lax.rsqrt(ms + EPS) * gain * sg_ref[ch["sl"], :].astype(F32)
        o_ref[ch["sl"], :] = y.astype(BF16)
        chunks[c] = None

    stages = [functools.partial(sums_stage, 0)]
    for c in range(nchunk):
        if c + 1 < nchunk:
            stages.append(functools.partial(sums_stage, c + 1))
        stages.append(functools.partial(level_stage, c))
        stages.append(functools.partial(state_stage, c))
        if c > 0:
            stages.append(functools.partial(norm_stage, c - 1))
        stages.append(functools.partial(value_stage, c))
    stages.append(functools.partial(norm_stage, nchunk - 1))
    return init, stages


def _ret_stages(lg_ref, q_ref, k_ref, v_ref, sg_ref, o_ref, s_ref, d_ref, qd_ref, kd_ref, cd_ref, nchunk):
    width = C_WIDTH
    cl = RET_CHUNK
    nhead = width // HEAD_DIM
    head_row = _iota((1, width), 1) >> 6

    def init():
        s_ref[...] = jnp.zeros_like(s_ref)
        lgl = jnp.zeros((1, width), F32)
        rel = _iota((cl, cl), 0) - _iota((cl, cl), 1)
        causal = rel >= 0
        relf = jnp.where(causal, rel, 0).astype(F32)
        for h in range(nhead):
            lgl = jnp.where(head_row == h, lg_ref[h], lgl)
            d_ref[:, cl * h:cl * (h + 1)] = jnp.where(causal, jnp.exp(relf * lg_ref[h]), 0.0)
        t = _iota((cl, width), 0).astype(F32)
        qd_ref[...] = jnp.exp((t + 1.0) * lgl)
        kd_ref[...] = jnp.exp((cl - 1.0 - t) * lgl)
        cd_ref[...] = jnp.exp(float(cl) * lgl)

    head = _iota((cl, width), 1) >> 6
    hmask = [(head == h).astype(BF16) for h in range(nhead)]
    same_head = (_iota((width, width), 0) >> 6) == (_iota((width, width), 1) >> 6)
    sel = same_head.astype(BF16)
    sel_t = ((_iota((width, nhead * cl), 0) >> 6)
             == (_iota((width, nhead * cl), 1) >> (cl.bit_length() - 1))).astype(BF16)
    chunks = [None] * nchunk

    def by_head(x):
        return jnp.concatenate([x * hmask[h] for h in range(nhead)], axis=0)

    def score_stage(c):
        sl = slice(c * cl, (c + 1) * cl)
        k4t = jnp.concatenate([k_ref[sl, :]] * nhead, axis=0).T
        chunks[c] = dict(sl=sl, scores=(_dot(q_ref[sl, :], k4t * sel_t) * d_ref[...]).astype(BF16))

    def state_stage(c):
        ch = chunks[c]
        sl = ch["sl"]
        st = s_ref[...]
        ch["o"] = _dot((q_ref[sl, :].astype(F32) * qd_ref[...]).astype(BF16), st.astype(BF16))
        kdec = (k_ref[sl, :].astype(F32) * kd_ref[...]).astype(BF16)
        s_ref[...] = st * cd_ref[...] + jnp.where(same_head, _dot_tn(kdec, v_ref[sl, :]), 0.0)

    def value_stage(c):
        ch = chunks[c]
        ch["o"] = ch["o"] + _dot(ch.pop("scores"), by_head(v_ref[ch["sl"], :]))

    def norm_stage(c):
        ch = chunks[c]
        o = ch["o"]
        ms = _dot((o * o).astype(BF16), sel) * (1.0 / HEAD_DIM)
        y = o * lax.rsqrt(ms + EPS) * sg_ref[ch["sl"], :].astype(F32)
        o_ref[ch["sl"], :] = y.astype(BF16)
        chunks[c] = None

    stages = [functools.partial(score_stage, 0)]
    for c in range(nchunk):
        if c + 1 < nchunk:
            stages.append(functools.partial(score_stage, c + 1))
        stages.append(functools.partial(state_stage, c))
        if c > 0:
            stages.append(functools.partial(norm_stage, c - 1))
        stages.append(functools.partial(value_stage, c))
    stages.append(functools.partial(norm_stage, nchunk - 1))
    return init, stages


def _swa_stages(sink_ref, q_ref, k_ref, v_ref, kp_ref, vp_ref, o_ref, seq_first, nblk):
    w = WINDOW
    not_first = jnp.logical_not(seq_first)
    low = _iota((2 * w, 128), 1) < HEAD_DIM
    upper = _iota((w, w), 1) > _iota((w, w), 0)
    upper_first = jnp.logical_and(upper, not_first)
    up_b = upper.astype(BF16)
    lo_b = jnp.logical_not(upper).astype(BF16)
    ones = jnp.ones((2 * w, 128), BF16)
    zero = jnp.zeros((2 * w, 128), BF16)
    second_head = _iota((2 * w, 1), 0) >= w
    upper2 = jnp.concatenate([upper, upper], axis=0)
    upper_first2 = jnp.concatenate([upper_first, upper_first], axis=0)
    up_b2 = jnp.concatenate([up_b, up_b], axis=0)
    lo_b2 = jnp.concatenate([lo_b, lo_b], axis=0)

    def operands(blk):
        cur = slice(blk * w, (blk + 1) * w)
        if blk == 0:
            kprev, vprev = kp_ref[...], vp_ref[...]
        else:
            prev = slice((blk - 1) * w, blk * w)
            kprev, vprev = k_ref[prev, :], v_ref[prev, :]
        kk = jnp.concatenate([kprev, k_ref[cur, :]], axis=0)
        vv = jnp.concatenate([vprev, v_ref[cur, :]], axis=0)
        kr = pltpu.roll(kk.astype(F32), HEAD_DIM, axis=1).astype(BF16)
        vr = pltpu.roll(vv.astype(F32), HEAD_DIM, axis=1).astype(BF16)
        ops = dict(cur=cur)
        for g in range(2):
            for half in range(2):
                keep = low if half == 0 else jnp.logical_not(low)
                ops["kt", g, half] = jnp.where(keep, kk if g == half else kr, zero).T
                ops["v", g, half] = jnp.concatenate([jnp.where(keep, vv if g == half else vr, zero), ones], axis=1)
        return ops

    items = [(blk, g, half) for blk in range(nblk) for g in range(2) for half in range(2)]
    ops_of = {}
    pending = {}
    state = {}

    def score_stage(n):
        blk, g, half = items[n]
        if blk not in ops_of:
            ops_of[blk] = operands(blk)
        cur = ops_of[blk]["cur"]
        q2 = jnp.concatenate([q_ref[cur, 256 * g:256 * g + 128], q_ref[cur, 256 * g + 128:256 * g + 256]], axis=0)
        both = _dot(q2, ops_of[blk]["kt", g, half])
        s_prev, s_cur = both[:, :w], both[:, w:]
        sink = jnp.where(second_head, sink_ref[4 * g + 2 + half], sink_ref[4 * g + half]) * LOG2E
        if blk == 0:
            s = jnp.where(upper_first2, s_prev, jnp.where(upper2, NEG_BIG, s_cur))
        else:
            s = jnp.where(upper2, s_prev, s_cur)
        m = jnp.maximum(jnp.max(s, axis=-1, keepdims=True), sink)
        p = jnp.exp2(s - m).astype(BF16)
        pending[n] = (jnp.concatenate([p * up_b2, p * lo_b2], axis=1), jnp.exp2(sink - m))

    def value_stage(n):
        blk, g, half = items[n]
        cur_ops = ops_of[blk]
        probs, sink_term = pending.pop(n)
        ol = _dot(probs, cur_ops["v", g, half])
        part = ol[:, :128] / (ol[:, 128:] + sink_term)
        if half == 0:
            state["acc"] = part
        else:
            both = (state.pop("acc") + part).astype(BF16)
            o_ref[cur_ops["cur"], 256 * g:256 * g + 128] = both[:w]
            o_ref[cur_ops["cur"], 256 * g + 128:256 * g + 256] = both[w:]

    stages = []
    for n in range(len(items) + SWA_LOOKAHEAD):
        if n < len(items):
            stages.append(functools.partial(score_stage, n))
        if n >= SWA_LOOKAHEAD:
            stages.append(functools.partial(value_stage, n - SWA_LOOKAHEAD))
    return stages


def _tail_stages(r_ref, mix_ref, p_ref, wo_ref, gf_ref, wg_ref, wu_ref, cw_ref, cb_ref, wd_ref,
                 gp_ref, wpg_ref, wpp_ref, gfin_ref, o_ref, lhs_ref, ge_ref, act_ref, halo_ref, seq_first, final):
    tm = r_ref.shape[0]
    val = {}

    def head_stage():
        r1 = r_ref[...] + _dot(mix_ref[...], wo_ref[...])
        halo = halo_ref[...]
        lhs_ref[0:CONV_HALO, :] = jnp.where(seq_first, jnp.zeros_like(halo), halo)
        h2 = _rms(r1, gf_ref[...]).astype(BF16)
        lhs_ref[CONV_HALO:, :] = h2
        halo_ref[...] = h2[tm - CONV_HALO:, :]
        val["r1"] = r1

    def mlp_stage(j):
        cols = slice(j * FF_SUB, (j + 1) * FF_SUB)
        ge = ge_ref.at[j % 2]
        ge[...] = _dot(lhs_ref[...], wg_ref[:, cols])
        cw = cw_ref[:, cols]
        gate = (ge[pl.ds(CONV_HALO - 2, tm), :] * cw[0:1, :]
                + ge[pl.ds(CONV_HALO - 1, tm), :] * cw[1:2, :]
                + ge[pl.ds(CONV_HALO, tm), :] * cw[2:3, :]
                + cb_ref[:, cols])
        up = _dot(lhs_ref[CONV_HALO:, :], wu_ref[:, cols])
        act_ref[:, cols] = (jax.nn.gelu(gate, approximate=True) * up).astype(BF16)

    def down_stage(n):
        cols = slice(n * FF_SUB, (n + 1) * FF_SUB)
        val["down", n] = _dot(act_ref[...], wd_ref[:, cols])

    def end_stage():
        r2 = val.pop("r1") + jnp.concatenate([val.pop(("down", n)) for n in range(D_MODEL // FF_SUB)], axis=1)
        gate = _sigmoid(_dot(_rms(r2, gp_ref[...]).astype(BF16), wpg_ref[...]))
        r3 = r2 + _dot(p_ref[...].astype(BF16), wpp_ref[...]) * gate
        o_ref[...] = _rms(r3, gfin_ref[...]) if final else r3

    stages = [head_stage]
    stages += [functools.partial(mlp_stage, j) for j in range(D_FF // FF_SUB)]
    stages += [functools.partial(down_stage, n) for n in range(D_MODEL // FF_SUB)]
    stages.append(end_stage)
    return stages


def _deal(parts, work=None):
    keyed = []
    for k, part in enumerate(parts):
        w = (work or {}).get(k, [1.0] * len(part))
        total, done = sum(w), 0.0
        for n, stage in enumerate(part):
            keyed.append(((done + 0.5 * w[n]) / total, k, n, stage))
            done += w[n]
    keyed.sort(key=lambda e: e[:3])
    return [e[3] for e in keyed]


def _body_kernel(sink_ref, lg_ref,
                 hq_ref, hk_ref, hv_ref, hsg_ref, lf_ref, gn_ref,
                 sq_ref, sk_ref, sv_ref, skp_ref, svp_ref,
                 rq_ref, rk_ref, rv_ref, rsg_ref,
                 r_ref, p_ref, wo_ref, gf_ref, wg_ref, wu_ref, cw_ref, cb_ref, wd_ref, gp_ref, wpg_ref, wpp_ref,
                 gfin_ref, o_ref,
                 mixnew_ref, mixold_ref, lhs_ref, ge_ref, act_ref, halo_ref,
                 hst_ref, m3_ref, lev_ref, rst_ref, d_ref, qd_ref, kd_ref, cd_ref,
                 *, ntiles, tiles_per_seq, final):
    tm = r_ref.shape[0]
    i = pl.program_id(0)
    mix_tile = jnp.minimum(i, ntiles - 1)
    mix_first = mix_tile % tiles_per_seq == 0
    tail_first = jnp.maximum(i - 1, 0) % tiles_per_seq == 0

    hgrn_init, hgrn = _hgrn_stages(hq_ref, hk_ref, hv_ref, hsg_ref, lf_ref, gn_ref,
                                   mixnew_ref.at[:, 0:A_WIDTH], hst_ref, m3_ref, lev_ref, tm // HGRN_CHUNK)
    ret_init, ret = _ret_stages(lg_ref, rq_ref, rk_ref, rv_ref, rsg_ref,
                                mixnew_ref.at[:, A_WIDTH + B_WIDTH:], rst_ref, d_ref, qd_ref, kd_ref, cd_ref,
                                tm // RET_CHUNK)
    swa = _swa_stages(sink_ref, sq_ref, sk_ref, sv_ref, skp_ref, svp_ref,
                      mixnew_ref.at[:, A_WIDTH:A_WIDTH + B_WIDTH], mix_first, tm // WINDOW)
    tail = _tail_stages(r_ref, mixold_ref, p_ref, wo_ref, gf_ref, wg_ref, wu_ref, cw_ref, cb_ref, wd_ref,
                        gp_ref, wpg_ref, wpp_ref, gfin_ref, o_ref, lhs_ref, ge_ref, act_ref, halo_ref,
                        tail_first, final)

    @pl.when(i == 0)
    def _():
        mixold_ref[...] = jnp.zeros_like(mixold_ref)
        halo_ref[...] = jnp.zeros_like(halo_ref)

    @pl.when(mix_first)
    def _():
        hgrn_init()
        ret_init()

    tail_work = [2.0] + [1.0] * (D_FF // FF_SUB) + [1.35] * (D_MODEL // FF_SUB) + [2.5]
    hgrn_cost = dict(sums_stage=1.0, level_stage=8.0, state_stage=3.0, norm_stage=1.0, value_stage=1.0)
    hgrn_work = [hgrn_cost[s.func.__name__] for s in hgrn]
    for stage in _deal([tail, hgrn, swa, ret], {0: tail_work, 1: hgrn_work}):
        stage()
    mixold_ref[...] = mixnew_ref[...]


def _layer_body(r, zt, lf, p, sinks, log_g, gnorm, w_out, ffn_gain, w_gate, w_up, conv_w, conv_b, w_down,
                ple_gain, w_pg, w_pp, final_gain, layer, seq_len, final):
    t = r.shape[0]
    tm = TOKEN_TILE
    ntiles = t // tm
    per = tm // WINDOW
    once = pl.Buffered(1)
    smem = pl.BlockSpec(memory_space=pltpu.SMEM)

    def mix_tile(i):
        return jnp.minimum(i, ntiles - 1)

    def tail_tile(i):
        return jnp.maximum(i - 1, 0)

    def zcol(width, c0):
        return pl.BlockSpec((tm, width), lambda i: (mix_tile(i), c0 // width))

    def zprev(c0):
        return pl.BlockSpec((WINDOW, B_KV_WIDTH),
                            lambda i: (jnp.maximum(mix_tile(i) * per - 1, 0), c0 // B_KV_WIDTH))

    def whole(shape):
        return pl.BlockSpec((None,) + shape, lambda i: (layer,) + (0,) * len(shape), pipeline_mode=once)

    return pl.pallas_call(
        functools.partial(_body_kernel, ntiles=ntiles, tiles_per_seq=seq_len // tm, final=final),
        grid=(ntiles + 1,),
        in_specs=[smem, smem,
                  zcol(A_WIDTH, AQ), zcol(A_WIDTH, AF), zcol(A_WIDTH, AI), zcol(A_WIDTH, AG),
                  pl.BlockSpec((tm, A_WIDTH), lambda i: (mix_tile(i), 0)),
                  whole((1, A_WIDTH)),
                  zcol(B_WIDTH, BQ), zcol(B_KV_WIDTH, BK), zcol(B_KV_WIDTH, BV), zprev(BK), zprev(BV),
                  zcol(C_WIDTH, CQ), zcol(C_WIDTH, CK), zcol(C_WIDTH, CV), zcol(C_WIDTH, CG),
                  pl.BlockSpec((tm, D_MODEL), lambda i: (tail_tile(i), 0)),
                  pl.BlockSpec((None, tm, PLE_DIM), lambda i: (layer, tail_tile(i), 0)),
                  whole((D_MODEL, D_MODEL)),
                  whole((1, D_MODEL)),
                  whole((D_MODEL, D_FF)),
                  whole((D_MODEL, D_FF)),
                  whole((3, D_FF)),
                  whole((1, D_FF)),
                  whole((D_FF, D_MODEL)),
                  whole((1, D_MODEL)),
                  whole((D_MODEL, D_MODEL)),
                  whole((PLE_DIM, D_MODEL)),
                  pl.BlockSpec((1, D_MODEL), lambda i: (0, 0), pipeline_mode=once)],
        out_specs=pl.BlockSpec((tm, D_MODEL), lambda i: (tail_tile(i), 0)),
        out_shape=jax.ShapeDtypeStruct((t, D_MODEL), F32),
        scratch_shapes=[pltpu.VMEM((tm, D_MODEL), BF16),
                        pltpu.VMEM((tm, D_MODEL), BF16),
                        pltpu.VMEM((tm + CONV_HALO, D_MODEL), BF16),
                        pltpu.VMEM((2, tm + CONV_HALO, FF_SUB), F32),
                        pltpu.VMEM((tm, D_FF), BF16),
                        pltpu.VMEM((CONV_HALO, D_MODEL), BF16),
                        pltpu.VMEM((A_WIDTH, A_WIDTH), F32),
                        pltpu.VMEM((2 * HGRN_CHUNK, 3 * HGRN_CHUNK), BF16),
                        pltpu.VMEM((HGRN_CHUNK, A_WIDTH), jnp.int32),
                        pltpu.VMEM((C_WIDTH, C_WIDTH), F32),
                        pltpu.VMEM((RET_CHUNK, C_WIDTH // HEAD_DIM * RET_CHUNK), F32),
                        pltpu.VMEM((RET_CHUNK, C_WIDTH), F32),
                        pltpu.VMEM((RET_CHUNK, C_WIDTH), F32),
                        pltpu.VMEM((1, C_WIDTH), F32)],
        compiler_params=_params("arbitrary"),
        name="layer_body",
    )(sinks, log_g,
      zt, zt, zt, zt, lf, gnorm,
      zt, zt, zt, zt, zt,
      zt, zt, zt, zt,
      r, p, w_out, ffn_gain, w_gate, w_up, conv_w, conv_b, w_down, ple_gain, w_pg, w_pp, final_gain)


def kernel(x, p, positions, attn_norm, w_in, hgrn_lb, hgrn_gnorm, attn_sinks, w_out, ffn_norm, w_gate, w_up,
           conv_w, conv_b, w_down, ple_norm, w_ple_gate, w_ple_proj, final_norm):
    b, s, d = x.shape
    depth = w_in.shape[0]
    t = b * s
    assert d == D_MODEL and s % TOKEN_TILE == 0 and t % INPROJ_TILE == 0
    assert TOKEN_TILE % RET_CHUNK == 0 and TOKEN_TILE % HGRN_CHUNK == 0 and TOKEN_TILE % WINDOW == 0

    vec = lambda a: a.reshape(depth, 1, a.shape[-1])
    bf = lambda a: a.astype(BF16)

    lb_all = pl.pallas_call(
        _lower_bound_kernel, out_shape=jax.ShapeDtypeStruct(hgrn_lb.shape, F32), name="hgrn_lower_bound",
    )(hgrn_lb.astype(F32))

    inv = 1.0 / (ROPE_THETA ** (jnp.arange(0, HEAD_DIM, 2, dtype=F32) / HEAD_DIM))
    inv = jnp.tile(inv, 4).reshape(1, 128)
    tt = INPROJ_TILE
    cos, sin_a, sin_b = pl.pallas_call(
        _rope_table_kernel,
        grid=(t // tt,),
        in_specs=[pl.BlockSpec((tt, 1), lambda i: (i, 0)), pl.BlockSpec((1, 128), lambda i: (0, 0))],
        out_specs=[pl.BlockSpec((tt, 128), lambda i: (i, 0))] * 3,
        out_shape=[jax.ShapeDtypeStruct((t, 128), F32)] * 3,
        compiler_params=_params("parallel"),
        name="rope_tables",
    )(positions.reshape(t, 1), inv)

    nhead_c = C_WIDTH // HEAD_DIM
    log_g = jnp.log(1.0 - 2.0 ** (-5.0 - jnp.arange(nhead_c, dtype=F32)))

    w_in_b, w_out_b, w_gate_b, w_up_b, w_down_b = bf(w_in), bf(w_out), bf(w_gate), bf(w_up), bf(w_down)
    w_pg_b, w_pp_b = bf(w_ple_gate), bf(w_ple_proj)
    attn_norm_v, ffn_norm_v, ple_norm_v = vec(attn_norm), vec(ffn_norm), vec(ple_norm)
    gnorm_v, conv_b_v, lb_v = vec(hgrn_gnorm), vec(conv_b), vec(lb_all)
    p_flat = p.reshape(depth, t, PLE_DIM)

    r = x.reshape(t, d).astype(F32)
    for i in range(depth):
        zt, lf = _inproj(r, attn_norm_v, w_in_b, cos, sin_a, sin_b, lb_v, i)
        r = _layer_body(r, zt, lf, p_flat, attn_sinks[i].astype(F32), log_g, gnorm_v, w_out_b, ffn_norm_v,
                        w_gate_b, w_up_b, conv_w, conv_b_v, w_down_b, ple_norm_v, w_pg_b, w_pp_b,
                        final_norm.reshape(1, d), i, s, i == depth - 1)
    return r.reshape(b, s, d).astype(x.dtype)
```
